```python
import math
import jax, jax.numpy as jnp
from jax import lax
import numpy as np

D_MODEL = 1024
BATCH = 1
SEQ = 16384
DEPTH = 1
DEC_BATCH = 8
DEC_SEQ = 64
PAST_LEN = 2048

CHUNK = 64
Q_BLOCK = 128
HEAD_DIM = 64
N_HEADS_FOX = 8
N_HEADS_DIFF = 4
FOX_WIDTH = N_HEADS_FOX * HEAD_DIM
DIFF_QK_WIDTH = N_HEADS_DIFF * 2 * HEAD_DIM
DIFF_V_DIM = 2 * HEAD_DIM
DIFF_V_WIDTH = N_HEADS_DIFF * DIFF_V_DIM
ROPE_THETA = 10000.0
NORM_EPS = 1e-6
SUBLN_EPS = 1e-5
PEER_HEADS = 8
PEER_N_KEYS = 128
PEER_N_EXPERTS = PEER_N_KEYS * PEER_N_KEYS
PEER_KEY_DIM = 128
PEER_TOPK = 16
PEER_BLOCK = 128
IN_SPLITS = (FOX_WIDTH, FOX_WIDTH, FOX_WIDTH, N_HEADS_FOX, DIFF_QK_WIDTH, DIFF_QK_WIDTH, DIFF_V_WIDTH, D_MODEL, D_MODEL)
IN_WIDTH = 3 * FOX_WIDTH + N_HEADS_FOX + 2 * DIFF_QK_WIDTH + DIFF_V_WIDTH + 2 * D_MODEL

kernel_name = "fox_diffattn_peer_streaming_step"


def _rmsnorm(x, g, eps=NORM_EPS):
    xf = x.astype(jnp.float32)
    y = xf * lax.rsqrt(jnp.mean(xf * xf, axis=-1, keepdims=True) + eps)
    return (y * g.astype(jnp.float32)).astype(x.dtype)


def _rope(x, pos):
    half = HEAD_DIM // 2
    inv = ROPE_THETA ** (-jnp.arange(half, dtype=jnp.float32) / half)
    ang = pos.astype(jnp.float32)[:, None] * inv[None, :]
    bshape = (1, x.shape[1]) + (1,) * (x.ndim - 3) + (half,)
    cos = jnp.cos(ang).reshape(bshape)
    sin = jnp.sin(ang).reshape(bshape)
    xf = x.astype(jnp.float32)
    x1, x2 = xf[..., :half], xf[..., half:]
    return jnp.concatenate([x1 * cos - x2 * sin, x2 * cos + x1 * sin], axis=-1).astype(x.dtype)


def _sweep(fn, args):
    T = args[0].shape[1]
    if T <= Q_BLOCK or T % Q_BLOCK:
        return fn(*args)
    nb = T // Q_BLOCK
    def split(a):
        return jnp.moveaxis(a.reshape((a.shape[0], nb, Q_BLOCK) + a.shape[2:]), 1, 0)
    out = lax.map(lambda blk: fn(*blk), tuple(split(a) for a in args))
    out = jnp.moveaxis(out, 0, 1)
    return out.reshape((out.shape[0], T) + out.shape[3:])


def _fox_attention(q, k, v, cum_q, cum_k, pos_q, pos_k):
    scale = HEAD_DIM ** -0.5
    kf = k.astype(jnp.float32)
    vf = v.astype(jnp.float32)
    ck = jnp.moveaxis(cum_k, 1, 2)[:, :, None, :]
    def block(qb, cqb, pqb):
        s = jnp.einsum('bqhd,bkhd->bhqk', qb.astype(jnp.float32), kf) * scale
        s = s + jnp.moveaxis(cqb, 1, 2)[..., None] - ck
        mask = pos_k[None, :] <= pqb[0][:, None]
        p = jax.nn.softmax(jnp.where(mask, s, -jnp.inf), axis=-1)
        return jnp.einsum('bhqk,bkhd->bqhd', p, vf).astype(v.dtype)
    return _sweep(block, (q, cum_q, pos_q))


def _diff_attention(q, k, v, lam, pos_q, pos_k):
    scale = HEAD_DIM ** -0.5
    kf = k.astype(jnp.float32)
    vf = v.astype(jnp.float32)
    chunk_k = pos_k // CHUNK
    def block(qb, pqb):
        s = jnp.einsum('bqhcd,bkhcd->bhcqk', qb.astype(jnp.float32), kf) * scale
        mask = chunk_k[None, :] <= (pqb[0] // CHUNK)[:, None]
        p = jax.nn.softmax(jnp.where(mask, s, -jnp.inf), axis=-1)
        pd = p[:, :, 0] - lam * p[:, :, 1]
        return jnp.einsum('bhqk,bkhe->bqhe', pd, vf).astype(v.dtype)
    return _sweep(block, (q, pos_q))


def _peer(h, wq, subkeys, u_tab, v_tab):
    B, T, D = h.shape
    n = B * T
    nb = -(-n // PEER_BLOCK)
    flat = jnp.pad(h.reshape(n, D), ((0, nb * PEER_BLOCK - n), (0, 0))).reshape(nb, PEER_BLOCK, D)
    def block(hb):
        q = (hb @ wq).reshape(PEER_BLOCK, PEER_HEADS, 2, PEER_KEY_DIM // 2)
        s = jnp.einsum('nhpd,hpkd->nhpk', q.astype(jnp.float32), subkeys.astype(jnp.float32))
        v1, i1 = lax.top_k(s[:, :, 0], PEER_TOPK)
        v2, i2 = lax.top_k(s[:, :, 1], PEER_TOPK)
        cand = (v1[..., :, None] + v2[..., None, :]).reshape(PEER_BLOCK, PEER_HEADS, PEER_TOPK * PEER_TOPK)
        cidx = (i1[..., :, None] * PEER_N_KEYS + i2[..., None, :]).reshape(PEER_BLOCK, PEER_HEADS, PEER_TOPK * PEER_TOPK)
        sc, sel = lax.top_k(cand, PEER_TOPK)
        eidx = jnp.take_along_axis(cidx, sel, axis=-1)
        g = jax.nn.softmax(sc, axis=-1)
        a = jax.nn.gelu(jnp.einsum('nhkd,nd->nhk', u_tab[eidx], hb).astype(jnp.float32), approximate=False)
        w = (g * a).astype(hb.dtype)
        return jnp.einsum('nhk,nhkd->nd', w, v_tab[eidx])
    out = lax.map(block, flat).reshape(nb * PEER_BLOCK, D)[:n]
    return out.reshape(B, T, D)


def _layer(x, c, past, p, lam_init):
    (w_ada, b_ada, g_attn, w_in, b_f, lq1, lk1, lq2, lk2, subln_g,
     w_pa, w_pb, w_o, g_ffn, peer_wq, peer_subkeys, peer_u, peer_v) = p
    past_fk, past_fv, past_lf, past_dk, past_dv = past
    B, T, _ = x.shape
    P = past_fk.shape[1]
    pos_k = jnp.arange(P + T, dtype=jnp.int32)
    pos_q = pos_k[P:][None, :]
    mod = (jax.nn.silu(c) @ w_ada + b_ada)[:, None, :]
    sh_a, sc_a, gt_a, sh_m, sc_m, gt_m = jnp.split(mod, 6, axis=-1)
    h = _rmsnorm(x, g_attn) * (1 + sc_a) + sh_a
    offs = np.cumsum(IN_SPLITS)[:-1].tolist()
    fq, fk, fv, ff, dq, dk, dv, ga, gb = jnp.split(h @ w_in, offs, axis=-1)
    fq = fq.reshape(B, T, N_HEADS_FOX, HEAD_DIM)
    fk = fk.reshape(B, T, N_HEADS_FOX, HEAD_DIM)
    fv = fv.reshape(B, T, N_HEADS_FOX, HEAD_DIM)
    logf = jax.nn.log_sigmoid((ff + b_f).astype(jnp.float32))
    K_f = jnp.concatenate([past_fk, fk], axis=1)
    V_f = jnp.concatenate([past_fv, fv], axis=1)
    cum = jnp.cumsum(jnp.concatenate([past_lf.astype(jnp.float32), logf], axis=1), axis=1)
    o_a = _fox_attention(fq, K_f, V_f, cum[:, P:], cum, pos_q, pos_k)
    dq = _rope(dq.reshape(B, T, N_HEADS_DIFF, 2, HEAD_DIM), pos_q[0])
    dk = _rope(dk.reshape(B, T, N_HEADS_DIFF, 2, HEAD_DIM), pos_q[0])
    dv = dv.reshape(B, T, N_HEADS_DIFF, DIFF_V_DIM)
    K_d = jnp.concatenate([past_dk, dk], axis=1)
    V_d = jnp.concatenate([past_dv, dv], axis=1)
    f32 = jnp.float32
    lam = (jnp.exp(jnp.sum(lq1.astype(f32) * lk1.astype(f32))) - jnp.exp(jnp.sum(lq2.astype(f32) * lk2.astype(f32))) + lam_init)
    o_b = _diff_attention(dq, K_d, V_d, lam, pos_q, pos_k)
    o_b = _rmsnorm(o_b, subln_g, SUBLN_EPS) * (1 - lam_init)
    y_a = o_a.reshape(B, T, FOX_WIDTH) @ w_pa
    y_b = o_b.reshape(B, T, DIFF_V_WIDTH) @ w_pb
    mix = (jax.nn.sigmoid(ga) * y_a + jax.nn.sigmoid(gb) * y_b) @ w_o
    x = x + gt_a * mix
    h2 = _rmsnorm(x, g_ffn) * (1 + sc_m) + sh_m
    x = x + gt_m * _peer(h2, peer_wq, peer_subkeys, peer_u, peer_v)
    return x, (fk, fv, logf.astype(x.dtype), dk, dv)


def setup_inputs(seed: int = 0) -> dict:
    key = jax.random.key(seed)
    ks = iter(jax.random.split(key, 40))
    f32 = jnp.float32
    def nrm(shape, scale=1.0):
        return scale * jax.random.normal(next(ks), shape, f32)
    D = D_MODEL
    return {
        "x_prompt": nrm((BATCH, SEQ, D)),
        "x_sample": nrm((DEC_BATCH, DEC_SEQ, D)),
        "cache_fox_k": nrm((DEPTH, DEC_BATCH, PAST_LEN, N_HEADS_FOX, HEAD_DIM)),
        "cache_fox_v": nrm((DEPTH, DEC_BATCH, PAST_LEN, N_HEADS_FOX, HEAD_DIM)),
        "cache_fox_logf": jax.nn.log_sigmoid(2.0 + nrm((DEPTH, DEC_BATCH, PAST_LEN, N_HEADS_FOX))),
        "cache_diff_k": nrm((DEPTH, DEC_BATCH, PAST_LEN, N_HEADS_DIFF, 2, HEAD_DIM)),
        "cache_diff_v": nrm((DEPTH, DEC_BATCH, PAST_LEN, N_HEADS_DIFF, DIFF_V_DIM)),
        "c_prompt": nrm((BATCH, D)),
        "c_sample": nrm((DEC_BATCH, D)),
        "w_ada": nrm((DEPTH, D, 6 * D), 0.5 * D ** -0.5),
        "b_ada": nrm((DEPTH, 6 * D), 0.1),
        "g_attn": 1.0 + nrm((DEPTH, D), 0.05),
        "w_in": nrm((DEPTH, D, IN_WIDTH), D ** -0.5),
        "b_f": 2.0 + nrm((DEPTH, N_HEADS_FOX), 0.1),
        "diff_lq1": nrm((DEPTH, HEAD_DIM), 0.1),
        "diff_lk1": nrm((DEPTH, HEAD_DIM), 0.1),
        "diff_lq2": nrm((DEPTH, HEAD_DIM), 0.1),
        "diff_lk2": nrm((DEPTH, HEAD_DIM), 0.1),
        "diff_subln_g": 1.0 + nrm((DEPTH, DIFF_V_DIM), 0.05),
        "w_pa": nrm((DEPTH, FOX_WIDTH, D), FOX_WIDTH ** -0.5),
        "w_pb": nrm((DEPTH, DIFF_V_WIDTH, D), DIFF_V_WIDTH ** -0.5),
        "w_o": nrm((DEPTH, D, D), D ** -0.5),
        "g_ffn": 1.0 + nrm((DEPTH, D), 0.05),
        "peer_wq": nrm((DEPTH, D, PEER_HEADS * PEER_KEY_DIM), D ** -0.5),
        "peer_subkeys": nrm((DEPTH, PEER_HEADS, 2, PEER_N_KEYS, PEER_KEY_DIM // 2), (PEER_KEY_DIM // 2) ** -0.5),
        "peer_u": nrm((DEPTH, PEER_N_EXPERTS, D), D ** -0.5),
        "peer_v": nrm((DEPTH, PEER_N_EXPERTS, D), PEER_HEADS ** -0.5),
        "g_final": 1.0 + nrm((D,), 0.05),
    }


def reference(x_prompt, x_sample, cache_fox_k, cache_fox_v, cache_fox_logf, cache_diff_k, cache_diff_v,
              c_prompt, c_sample, w_ada, b_ada, g_attn, w_in, b_f, diff_lq1, diff_lk1, diff_lq2, diff_lk2,
              diff_subln_g, w_pa, w_pb, w_o, g_ffn, peer_wq, peer_subkeys, peer_u, peer_v, g_final):
    dt = x_prompt.dtype
    empty = (jnp.zeros((BATCH, 0, N_HEADS_FOX, HEAD_DIM), dt),
             jnp.zeros((BATCH, 0, N_HEADS_FOX, HEAD_DIM), dt),
             jnp.zeros((BATCH, 0, N_HEADS_FOX), dt),
             jnp.zeros((BATCH, 0, N_HEADS_DIFF, 2, HEAD_DIM), dt),
             jnp.zeros((BATCH, 0, N_HEADS_DIFF, DIFF_V_DIM), dt))
    yp, ys = x_prompt, x_sample
    sp_all = [[], [], [], [], []]
    ss_all = [[], [], [], [], []]
    for l in range(DEPTH):
        lam_init = 0.8 - 0.6 * math.exp(-0.3 * l)
        p = (w_ada[l], b_ada[l], g_attn[l], w_in[l], b_f[l], diff_lq1[l], diff_lk1[l], diff_lq2[l], diff_lk2[l],
             diff_subln_g[l], w_pa[l], w_pb[l], w_o[l], g_ffn[l], peer_wq[l], peer_subkeys[l], peer_u[l], peer_v[l])
        yp, sp = _layer(yp, c_prompt, empty, p, lam_init)
        past = (cache_fox_k[l], cache_fox_v[l], cache_fox_logf[l], cache_diff_k[l], cache_diff_v[l])
        ys, ss = _layer(ys, c_sample, past, p, lam_init)
        for i in range(5):
            sp_all[i].append(sp[i])
            ss_all[i].append(ss[i])
    y_prompt = _rmsnorm(yp, g_final)
    y_sample = _rmsnorm(ys, g_final)
    new_fox_k_prompt = jnp.stack(sp_all[0])
    new_fox_v_prompt = jnp.stack(sp_all[1])
    new_fox_logf_prompt = jnp.stack(sp_all[2])
    new_diff_k_prompt = jnp.stack(sp_all[3])
    new_diff_v_prompt = jnp.stack(sp_all[4])
    new_fox_k_sample = jnp.stack(ss_all[0])
    new_fox_v_sample = jnp.stack(ss_all[1])
    new_fox_logf_sample = jnp.stack(ss_all[2])
    new_diff_k_sample = jnp.stack(ss_all[3])
    new_diff_v_sample = jnp.stack(ss_all[4])
    return (y_prompt, y_sample, new_fox_k_prompt, new_fox_v_prompt, new_fox_logf_prompt, new_diff_k_prompt,
            new_diff_v_prompt, new_fox_k_sample, new_fox_v_sample, new_fox_logf_sample, new_diff_k_sample,
            new_diff_v_sample)
```

```python
import functools
import math

import numpy as np
import jax
import jax.numpy as jnp
from jax import lax
from jax.experimental import pallas as pl
from jax.experimental.pallas import tpu as pltpu

F32 = jnp.float32
BF16 = jnp.bfloat16
HIGHEST = lax.Precision.HIGHEST

D_MODEL = 1024
CHUNK = 64
HEAD_DIM = 64
N_HEADS_FOX = 8
N_HEADS_DIFF = 4
FOX_WIDTH = N_HEADS_FOX * HEAD_DIM
DIFF_QK_WIDTH = N_HEADS_DIFF * 2 * HEAD_DIM
DIFF_V_DIM = 2 * HEAD_DIM
DIFF_V_WIDTH = N_HEADS_DIFF * DIFF_V_DIM
ROPE_THETA = 10000.0
NORM_EPS = 1e-6
SUBLN_EPS = 1e-5
PEER_HEADS = 8
PEER_N_KEYS = 128
PEER_KEY_HALF = 64
PEER_TOPK = 16
LAM_INIT = 0.8 - 0.6 * math.exp(-0.3 * 0)
ATTN_SCALE = HEAD_DIM ** -0.5

LANES = 128
SUBLANES = 8
VMEM_LIMIT_BYTES = 48 * 1024 * 1024

ATTN_BLOCK = 512
TOKEN_BLOCK = 512
PEER_TOKEN_BLOCK = 512
PEER_EXPERT_BLOCK = 512
N_EXTRACT = PEER_TOPK + 1
VBUF_ROWS = 24

Q_BIAS_COL = HEAD_DIM
Q_ONE_COL = HEAD_DIM + 3
V_ONE_COL = HEAD_DIM


def _cparams(sem):
    return pltpu.CompilerParams(dimension_semantics=sem, vmem_limit_bytes=VMEM_LIMIT_BYTES)


def _log_sigmoid(x):
    return jnp.minimum(x, 0.0) - jnp.log1p(jnp.exp(-jnp.abs(x)))


def _modnorm(x, g, sc, sh):
    ms = jnp.mean(x * x, axis=-1, keepdims=True)
    return (x * lax.rsqrt(ms + NORM_EPS) * g) * (1.0 + sc) + sh


def _split3(a):
    hi = a.astype(BF16)
    r = a - hi.astype(F32)
    mid = r.astype(BF16)
    lo = (r - mid.astype(F32)).astype(BF16)
    return jnp.concatenate([hi, mid, lo], axis=1)


def _incl_cumsum_rows(v):
    n = v.shape[0]
    row = lax.broadcasted_iota(jnp.int32, (n, n), 0)
    col = lax.broadcasted_iota(jnp.int32, (n, n), 1)
    tri = (row >= col).astype(F32)
    return jnp.dot(tri, v, precision=HIGHEST, preferred_element_type=F32)


def _ada_kernel(c_ref, w_ref, b_ref, o_ref):
    c = c_ref[...]
    s = c * jax.nn.sigmoid(c)
    o_ref[...] = jnp.dot(s, w_ref[...], precision=HIGHEST, preferred_element_type=F32) + b_ref[...]


def _ada_call(c_rows, w_ada, b_ada):
    n = c_rows.shape[0]
    return pl.pallas_call(
        _ada_kernel,
        out_shape=jax.ShapeDtypeStruct((n, 6 * D_MODEL), F32),
        grid=(6,),
        in_specs=[
            pl.BlockSpec((n, D_MODEL), lambda j: (0, 0)),
            pl.BlockSpec((D_MODEL, D_MODEL), lambda j: (0, j)),
            pl.BlockSpec((1, D_MODEL), lambda j: (0, j)),
        ],
        out_specs=pl.BlockSpec((n, D_MODEL), lambda j: (0, j)),
        compiler_params=_cparams(("arbitrary",)),
        name="ada_mod",
    )(c_rows, w_ada, b_ada)


def _fox_aug_consts():
    place = np.zeros((FOX_WIDTH, N_HEADS_FOX * LANES), np.float32)
    for h in range(N_HEADS_FOX):
        for d in range(HEAD_DIM):
            place[h * HEAD_DIM + d, h * LANES + d] = 1.0
    sel_q = np.zeros((3 * LANES, N_HEADS_FOX * LANES), np.float32)
    sel_k = np.zeros((3 * LANES, N_HEADS_FOX * LANES), np.float32)
    one_q = np.zeros((1, N_HEADS_FOX * LANES), np.float32)
    one_k = np.zeros((1, N_HEADS_FOX * LANES), np.float32)
    one_v = np.zeros((1, N_HEADS_FOX * LANES), np.float32)
    for h in range(N_HEADS_FOX):
        for piece in range(3):
            sel_q[piece * LANES + h, h * LANES + Q_BIAS_COL + piece] = 1.0
            sel_k[piece * LANES + h, h * LANES + Q_ONE_COL + piece] = -1.0
            one_q[0, h * LANES + Q_ONE_COL + piece] = 1.0
            one_k[0, h * LANES + Q_BIAS_COL + piece] = 1.0
        one_v[0, h * LANES + V_ONE_COL] = 1.0
    return (jnp.asarray(place, BF16), jnp.asarray(sel_q, BF16), jnp.asarray(sel_k, BF16),
            jnp.asarray(one_q), jnp.asarray(one_k), jnp.asarray(one_v))


def _store_heads(ref, wide, n_heads, width):
    for h in range(n_heads):
        ref[h] = wide[:, h * width:(h + 1) * width]


def _fox_pre_kernel(x_ref, sh_ref, sc_ref, g_ref, w_ref, wff_ref, bf_ref, place_ref, selq_ref, selk_ref,
                    oneq_ref, onek_ref, onev_ref,
                    fk_ref, fv_ref, logf_ref, qa_ref, ka_ref, va_ref, base_ref,
                    carry_ref, *, carry):
    if carry:
        @pl.when(pl.program_id(0) == 0)
        def _():
            carry_ref[...] = jnp.zeros(carry_ref.shape, F32)
    else:
        carry_ref[...] = jnp.zeros(carry_ref.shape, F32)

    h = _modnorm(x_ref[...], g_ref[...], sc_ref[...], sh_ref[...])
    proj = jnp.dot(h.astype(BF16), w_ref[...], preferred_element_type=F32)
    fq = proj[:, :FOX_WIDTH]
    fk = proj[:, FOX_WIDTH:2 * FOX_WIDTH]
    fv = proj[:, 2 * FOX_WIDTH:]
    fk_ref[...] = fk
    fv_ref[...] = fv

    ff = jnp.dot(h, wff_ref[...], precision=HIGHEST, preferred_element_type=F32) + bf_ref[...]
    lane = lax.broadcasted_iota(jnp.int32, ff.shape, 1)
    logf = jnp.where(lane < N_HEADS_FOX, _log_sigmoid(ff), 0.0)
    logf_ref[...] = logf[:, :N_HEADS_FOX]

    a = _incl_cumsum_rows(logf)
    a3 = _split3(a)
    place = place_ref[...]
    qa = (jnp.dot(fq.astype(BF16), place, preferred_element_type=F32)
          + jnp.dot(a3, selq_ref[...], preferred_element_type=F32) + oneq_ref[...])
    ka = (jnp.dot(fk.astype(BF16), place, preferred_element_type=F32)
          + jnp.dot(a3, selk_ref[...], preferred_element_type=F32) + onek_ref[...])
    va = jnp.dot(fv.astype(BF16), place, preferred_element_type=F32) + onev_ref[...]
    _store_heads(qa_ref, qa.astype(BF16), N_HEADS_FOX, LANES)
    _store_heads(ka_ref, ka.astype(BF16), N_HEADS_FOX, LANES)
    _store_heads(va_ref, va.astype(BF16), N_HEADS_FOX, LANES)

    base_ref[...] = carry_ref[...]
    carry_ref[...] = carry_ref[...] + a[a.shape[0] - 1:, :]


def _fox_pre_call(x2d, sh, sc, g, w_fox, w_ff, b_f, consts, *, tb, per_block_mod, carry):
    n = x2d.shape[0]
    nb = n // tb
    place, sel_q, sel_k, one_q, one_k, one_v = consts
    mod_map = (lambda i: (i, 0, 0)) if per_block_mod else (lambda i: (0, 0, 0))
    const2 = lambda i: (0, 0)
    hm = N_HEADS_FOX
    return pl.pallas_call(
        functools.partial(_fox_pre_kernel, carry=carry),
        out_shape=(
            jax.ShapeDtypeStruct((n, FOX_WIDTH), F32),
            jax.ShapeDtypeStruct((n, FOX_WIDTH), F32),
            jax.ShapeDtypeStruct((n, N_HEADS_FOX), F32),
            jax.ShapeDtypeStruct((hm, n, LANES), BF16),
            jax.ShapeDtypeStruct((hm, n, LANES), BF16),
            jax.ShapeDtypeStruct((hm, n, LANES), BF16),
            jax.ShapeDtypeStruct((nb, 1, LANES), F32),
        ),
        grid=(nb,),
        in_specs=[
            pl.BlockSpec((tb, D_MODEL), lambda i: (i, 0)),
            pl.BlockSpec((None, 1, D_MODEL), mod_map),
            pl.BlockSpec((None, 1, D_MODEL), mod_map),
            pl.BlockSpec((1, D_MODEL), const2),
            pl.BlockSpec(w_fox.shape, const2),
            pl.BlockSpec(w_ff.shape, const2),
            pl.BlockSpec(b_f.shape, const2),
            pl.BlockSpec(place.shape, const2),
            pl.BlockSpec(sel_q.shape, const2),
            pl.BlockSpec(sel_k.shape, const2),
            pl.BlockSpec(one_q.shape, const2),
            pl.BlockSpec(one_k.shape, const2),
            pl.BlockSpec(one_v.shape, const2),
        ],
        out_specs=(
            pl.BlockSpec((tb, FOX_WIDTH), lambda i: (i, 0)),
            pl.BlockSpec((tb, FOX_WIDTH), lambda i: (i, 0)),
            pl.BlockSpec((tb, N_HEADS_FOX), lambda i: (i, 0)),
            pl.BlockSpec((hm, tb, LANES), lambda i: (0, i, 0)),
            pl.BlockSpec((hm, tb, LANES), lambda i: (0, i, 0)),
            pl.BlockSpec((hm, tb, LANES), lambda i: (0, i, 0)),
            pl.BlockSpec((None, 1, LANES), lambda i: (i, 0, 0)),
        ),
        scratch_shapes=[pltpu.VMEM((1, LANES), F32)],
        compiler_params=_cparams(("arbitrary",)),
        name="fox_pre",
    )(x2d, sh, sc, g, w_fox, w_ff, b_f, place, sel_q, sel_k, one_q, one_k, one_v)


def _fox_cache_kernel(ck_ref, cv_ref, clf_ref, place_ref, selk_ref, onek_ref, onev_ref,
                      ka_ref, va_ref, base_ref, tot_ref, carry_ref):
    @pl.when(pl.program_id(1) == 0)
    def _():
        carry_ref[...] = jnp.zeros(carry_ref.shape, F32)

    a = _incl_cumsum_rows(clf_ref[...])
    a3 = _split3(a)
    place = place_ref[...]
    ka = (jnp.dot(ck_ref[...].astype(BF16), place, preferred_element_type=F32)
          + jnp.dot(a3, selk_ref[...], preferred_element_type=F32) + onek_ref[...])
    va = jnp.dot(cv_ref[...].astype(BF16), place, preferred_element_type=F32) + onev_ref[...]
    _store_heads(ka_ref, ka.astype(BF16), N_HEADS_FOX, LANES)
    _store_heads(va_ref, va.astype(BF16), N_HEADS_FOX, LANES)
    tot = a[a.shape[0] - 1:, :]
    base_ref[...] = carry_ref[...]
    tot_ref[...] = tot
    carry_ref[...] = carry_ref[...] + tot


def _fox_cache_call(ck, cv, clf_pad, consts):
    nbatch, plen, _ = ck.shape
    tb = ATTN_BLOCK
    nr = plen // tb
    place, _, sel_k, _, one_k, one_v = consts
    const2 = lambda b, r: (0, 0)
    hm = N_HEADS_FOX
    return pl.pallas_call(
        _fox_cache_kernel,
        out_shape=(
            jax.ShapeDtypeStruct((nbatch, hm, plen, LANES), BF16),
            jax.ShapeDtypeStruct((nbatch, hm, plen, LANES), BF16),
            jax.ShapeDtypeStruct((nbatch, nr, 1, LANES), F32),
            jax.ShapeDtypeStruct((nbatch, nr, 1, LANES), F32),
        ),
        grid=(nbatch, nr),
        in_specs=[
            pl.BlockSpec((None, tb, FOX_WIDTH), lambda b, r: (b, r, 0)),
            pl.BlockSpec((None, tb, FOX_WIDTH), lambda b, r: (b, r, 0)),
            pl.BlockSpec((None, tb, LANES), lambda b, r: (b, r, 0)),
            pl.BlockSpec(place.shape, const2),
            pl.BlockSpec(sel_k.shape, const2),
            pl.BlockSpec(one_k.shape, const2),
            pl.BlockSpec(one_v.shape, const2),
        ],
        out_specs=(
            pl.BlockSpec((None, hm, tb, LANES), lambda b, r: (b, 0, r, 0)),
            pl.BlockSpec((None, hm, tb, LANES), lambda b, r: (b, 0, r, 0)),
            pl.BlockSpec((None, None, 1, LANES), lambda b, r: (b, r, 0, 0)),
            pl.BlockSpec((None, None, 1, LANES), lambda b, r: (b, r, 0, 0)),
        ),
        scratch_shapes=[pltpu.VMEM((1, LANES), F32)],
        compiler_params=_cparams(("arbitrary", "arbitrary")),
        name="fox_cache_prep",
    )(ck, cv, clf_pad, place, sel_k, one_k, one_v)


def _rope_lanes(x, cos, sin_signed):
    n = x.shape[1]
    reps = n // LANES
    cos_w = jnp.tile(cos, (1, reps))
    sin_w = jnp.tile(sin_signed, (1, reps))
    lane = lax.broadcasted_iota(jnp.int32, x.shape, 1)
    first_half = (lane % HEAD_DIM) < (HEAD_DIM // 2)
    rot = jnp.where(first_half, pltpu.roll(x, n - HEAD_DIM // 2, 1), pltpu.roll(x, HEAD_DIM // 2, 1))
    return x * cos_w + rot * sin_w


def _diff_pre_kernel(x_ref, sh_ref, sc_ref, g_ref, w_ref, cos_ref, sin_ref,
                     dk_ref, dv_ref, qp_ref, kb_ref, va_ref, sga_ref, sgb_ref):
    h = _modnorm(x_ref[...], g_ref[...], sc_ref[...], sh_ref[...])
    proj = jnp.dot(h.astype(BF16), w_ref[...], preferred_element_type=F32)
    qw = 2 * DIFF_QK_WIDTH
    dq_pad = proj[:, :qw]
    dk = proj[:, qw:qw + DIFF_QK_WIDTH]
    dv = proj[:, qw + DIFF_QK_WIDTH:qw + DIFF_QK_WIDTH + DIFF_V_WIDTH]
    gates = proj[:, qw + DIFF_QK_WIDTH + DIFF_V_WIDTH:]
    cos = cos_ref[...]
    sin = sin_ref[...]
    dq_rot = _rope_lanes(dq_pad, cos, sin)
    dk_rot = _rope_lanes(dk, cos, sin)
    dk_ref[...] = dk_rot
    dv_ref[...] = dv
    _store_heads(qp_ref, dq_rot.astype(BF16), 2 * N_HEADS_DIFF, LANES)
    _store_heads(kb_ref, dk_rot.astype(BF16), N_HEADS_DIFF, LANES)
    tb = dv.shape[0]
    lane = lax.broadcasted_iota(jnp.int32, (tb, LANES), 1)
    ones_blk = (lane == 0).astype(BF16)
    dvb = dv.astype(BF16)
    for hd in range(N_HEADS_DIFF):
        va_ref[hd] = jnp.concatenate([dvb[:, hd * DIFF_V_DIM:(hd + 1) * DIFF_V_DIM], ones_blk], axis=1)
    sga_ref[...] = jax.nn.sigmoid(gates[:, :D_MODEL]).astype(BF16)
    sgb_ref[...] = jax.nn.sigmoid(gates[:, D_MODEL:]).astype(BF16)


def _diff_pre_call(x2d, sh, sc, g, w_dg, cos_t, sin_t, *, tb, per_block_mod):
    n = x2d.shape[0]
    nb = n // tb
    mod_map = (lambda i: (i, 0, 0)) if per_block_mod else (lambda i: (0, 0, 0))
    const2 = lambda i: (0, 0)
    row = lambda i: (i, 0)
    return pl.pallas_call(
        _diff_pre_kernel,
        out_shape=(
            jax.ShapeDtypeStruct((n, DIFF_QK_WIDTH), F32),
            jax.ShapeDtypeStruct((n, DIFF_V_WIDTH), F32),
            jax.ShapeDtypeStruct((2 * N_HEADS_DIFF, n, LANES), BF16),
            jax.ShapeDtypeStruct((N_HEADS_DIFF, n, LANES), BF16),
            jax.ShapeDtypeStruct((N_HEADS_DIFF, n, 2 * LANES), BF16),
            jax.ShapeDtypeStruct((n, D_MODEL), BF16),
            jax.ShapeDtypeStruct((n, D_MODEL), BF16),
        ),
        grid=(nb,),
        in_specs=[
            pl.BlockSpec((tb, D_MODEL), row),
            pl.BlockSpec((None, 1, D_MODEL), mod_map),
            pl.BlockSpec((None, 1, D_MODEL), mod_map),
            pl.BlockSpec((1, D_MODEL), const2),
            pl.BlockSpec(w_dg.shape, const2),
            pl.BlockSpec((tb, LANES), row),
            pl.BlockSpec((tb, LANES), row),
        ],
        out_specs=(
            pl.BlockSpec((tb, DIFF_QK_WIDTH), row),
            pl.BlockSpec((tb, DIFF_V_WIDTH), row),
            pl.BlockSpec((2 * N_HEADS_DIFF, tb, LANES), lambda i: (0, i, 0)),
            pl.BlockSpec((N_HEADS_DIFF, tb, LANES), lambda i: (0, i, 0)),
            pl.BlockSpec((N_HEADS_DIFF, tb, 2 * LANES), lambda i: (0, i, 0)),
            pl.BlockSpec((tb, D_MODEL), row),
            pl.BlockSpec((tb, D_MODEL), row),
        ),
        compiler_params=_cparams(("arbitrary",)),
        name="diff_pre",
    )(x2d, sh, sc, g, w_dg, cos_t, sin_t)


def _attn_pairs(nq, tq, tk, qoff):
    ii, jj, last = [], [], []
    for i in range(nq):
        jmax = (qoff + i * tq + tq - 1) // tk
        for j in range(jmax + 1):
            ii.append(i)
            jj.append(j)
            last.append(1 if j == jmax else 0)
    return (jnp.asarray(ii, jnp.int32), jnp.asarray(jj, jnp.int32), jnp.asarray(last, jnp.int32))


_NT_DIMS = (((1,), (1,)), ((), ()))


def _fox_attn_kernel(ii_ref, jj_ref, last_ref, qb_ref, kb_ref, q_ref, k_ref, v_ref, o_ref, m_ref, acc_ref,
                     *, tq, tk, nq, nk, qoff):
    nh = N_HEADS_FOX
    b = pl.program_id(0)
    p = pl.program_id(1)
    i = ii_ref[p]
    j = jj_ref[p]
    q_lo = qoff + i * tq

    @pl.when(j == 0)
    def _init():
        m_ref[...] = jnp.full(m_ref.shape, -jnp.inf, F32)
        acc_ref[...] = jnp.zeros(acc_ref.shape, F32)

    def head_step(h, masked):
        d = qb_ref[(b * nh + h) * nq + i] - kb_ref[(b * nh + h) * nk + j]
        s = lax.dot_general(q_ref[h], k_ref[h], _NT_DIMS, preferred_element_type=F32)
        if masked:
            row = lax.broadcasted_iota(jnp.int32, (tq, tk), 0) + q_lo
            col = lax.broadcasted_iota(jnp.int32, (tq, tk), 1) + j * tk
            s = jnp.where(col <= row, s, -jnp.inf)
        m_prev = m_ref[h]
        m_new = jnp.maximum(m_prev, jnp.max(s, axis=1, keepdims=True) + d)
        alpha = jnp.exp(m_prev - m_new)
        pexp = jnp.exp(s - (m_new - d))
        acc_ref[h] = alpha * acc_ref[h] + jnp.dot(pexp.astype(BF16), v_ref[h], preferred_element_type=F32)
        m_ref[h] = m_new

    def run(masked):
        def body(h, c):
            head_step(h, masked)
            return c
        lax.fori_loop(0, nh, body, 0)

    is_diag = j * tk + tk - 1 > q_lo

    @pl.when(is_diag)
    def _():
        run(True)

    @pl.when(jnp.logical_not(is_diag))
    def _():
        run(False)

    @pl.when(last_ref[p] == 1)
    def _fin():
        def body(h, c):
            acc = acc_ref[h]
            o_ref[h] = (acc * (1.0 / acc[:, V_ONE_COL:V_ONE_COL + 1])).astype(BF16)
            return c
        lax.fori_loop(0, nh, body, 0)


def _fox_attn_call(q, k, v, qbase, kbase, *, tq, tk, qoff):
    nbatch, nh, lq, _ = q.shape
    lk = k.shape[2]
    nq = lq // tq
    nk = lk // tk
    ii, jj, last = _attn_pairs(nq, tq, tk, qoff)
    npairs = ii.shape[0]
    grid_spec = pltpu.PrefetchScalarGridSpec(
        num_scalar_prefetch=5,
        grid=(nbatch, npairs),
        in_specs=[
            pl.BlockSpec((None, nh, tq, LANES), lambda b, p, ii, jj, la, qb, kb: (b, 0, ii[p], 0)),
            pl.BlockSpec((None, nh, tk, LANES), lambda b, p, ii, jj, la, qb, kb: (b, 0, jj[p], 0)),
            pl.BlockSpec((None, nh, tk, LANES), lambda b, p, ii, jj, la, qb, kb: (b, 0, jj[p], 0)),
        ],
        out_specs=pl.BlockSpec((None, nh, tq, LANES), lambda b, p, ii, jj, la, qb, kb: (b, 0, ii[p], 0)),
        scratch_shapes=[pltpu.VMEM((nh, tq, 1), F32), pltpu.VMEM((nh, tq, LANES), F32)],
    )
    return pl.pallas_call(
        functools.partial(_fox_attn_kernel, tq=tq, tk=tk, nq=nq, nk=nk, qoff=qoff),
        out_shape=jax.ShapeDtypeStruct((nbatch, nh, lq, LANES), BF16),
        grid_spec=grid_spec,
        compiler_params=_cparams(("arbitrary", "arbitrary")),
        name="fox_attn",
    )(ii, jj, last, qbase, kbase, q, k, v)


def _diff_attn_kernel(ii_ref, jj_ref, last_ref, q_ref, k_ref, v_ref, lq1_ref, lk1_ref, lq2_ref, lk2_ref, g_ref,
                      o_ref, m_ref, acc_ref, *, tq, tk, qoff):
    nh = N_HEADS_DIFF
    p = pl.program_id(1)
    i = ii_ref[p]
    j = jj_ref[p]
    q_lo = qoff + i * tq

    @pl.when(j == 0)
    def _init():
        m_ref[...] = jnp.full(m_ref.shape, -jnp.inf, F32)
        acc_ref[...] = jnp.zeros(acc_ref.shape, F32)

    def head_step(h, masked):
        kk = k_ref[h]
        vv = v_ref[h]
        if masked:
            row = (lax.broadcasted_iota(jnp.int32, (tq, tk), 0) + q_lo) // CHUNK
            col = (lax.broadcasted_iota(jnp.int32, (tq, tk), 1) + j * tk) // CHUNK
            visible = col <= row
        for c in range(2):
            hc = 2 * h + c
            s = lax.dot_general(q_ref[hc], kk, _NT_DIMS, preferred_element_type=F32)
            if masked:
                s = jnp.where(visible, s, -jnp.inf)
            m_prev = m_ref[hc]
            m_new = jnp.maximum(m_prev, jnp.max(s, axis=1, keepdims=True))
            alpha = jnp.exp(m_prev - m_new)
            pexp = jnp.exp(s - m_new)
            acc_ref[hc] = alpha * acc_ref[hc] + jnp.dot(pexp.astype(BF16), vv, preferred_element_type=F32)
            m_ref[hc] = m_new

    def run(masked):
        def body(h, c):
            head_step(h, masked)
            return c
        lax.fori_loop(0, nh, body, 0)

    is_diag = (j * tk + tk - 1) // CHUNK > q_lo // CHUNK

    @pl.when(is_diag)
    def _():
        run(True)

    @pl.when(jnp.logical_not(is_diag))
    def _():
        run(False)

    @pl.when(last_ref[p] == 1)
    def _fin():
        lam = (jnp.exp(jnp.sum(lq1_ref[...] * lk1_ref[...], axis=1, keepdims=True))
               - jnp.exp(jnp.sum(lq2_ref[...] * lk2_ref[...], axis=1, keepdims=True)) + LAM_INIT)
        g = g_ref[...] * (1.0 - LAM_INIT)

        def body(h, c):
            a1 = acc_ref[2 * h]
            a2 = acc_ref[2 * h + 1]
            o1 = a1[:, :DIFF_V_DIM] * (1.0 / a1[:, DIFF_V_DIM:DIFF_V_DIM + 1])
            o2 = a2[:, :DIFF_V_DIM] * (1.0 / a2[:, DIFF_V_DIM:DIFF_V_DIM + 1])
            o = o1 - lam * o2
            ms = jnp.mean(o * o, axis=-1, keepdims=True)
            o_ref[h] = (o * lax.rsqrt(ms + SUBLN_EPS) * g).astype(BF16)
            return c
        lax.fori_loop(0, nh, body, 0)


def _diff_attn_call(q, k, v, lq1, lk1, lq2, lk2, subln_g, *, tq, tk, qoff):
    nbatch, _, lq, _ = q.shape
    lk = k.shape[2]
    nq = lq // tq
    nh = N_HEADS_DIFF
    ii, jj, last = _attn_pairs(nq, tq, tk, qoff)
    npairs = ii.shape[0]
    small = lambda b, p, ii, jj, la: (0, 0)
    grid_spec = pltpu.PrefetchScalarGridSpec(
        num_scalar_prefetch=3,
        grid=(nbatch, npairs),
        in_specs=[
            pl.BlockSpec((None, 2 * nh, tq, LANES), lambda b, p, ii, jj, la: (b, 0, ii[p], 0)),
            pl.BlockSpec((None, nh, tk, LANES), lambda b, p, ii, jj, la: (b, 0, jj[p], 0)),
            pl.BlockSpec((None, nh, tk, 2 * LANES), lambda b, p, ii, jj, la: (b, 0, jj[p], 0)),
            pl.BlockSpec((1, HEAD_DIM), small),
            pl.BlockSpec((1, HEAD_DIM), small),
            pl.BlockSpec((1, HEAD_DIM), small),
            pl.BlockSpec((1, HEAD_DIM), small),
            pl.BlockSpec((1, DIFF_V_DIM), small),
        ],
        out_specs=pl.BlockSpec((None, nh, tq, LANES), lambda b, p, ii, jj, la: (b, 0, ii[p], 0)),
        scratch_shapes=[pltpu.VMEM((2 * nh, tq, 1), F32), pltpu.VMEM((2 * nh, tq, 2 * LANES), F32)],
    )
    return pl.pallas_call(
        functools.partial(_diff_attn_kernel, tq=tq, tk=tk, qoff=qoff),
        out_shape=jax.ShapeDtypeStruct((nbatch, nh, lq, LANES), BF16),
        grid_spec=grid_spec,
        compiler_params=_cparams(("arbitrary", "arbitrary")),
        name="diff_attn",
    )(ii, jj, last, q, k, v, lq1, lk1, lq2, lk2, subln_g)


def _post_kernel(oa_ref, ob_ref, sga_ref, sgb_ref, x_ref, gta_ref, shm_ref, scm_ref, g_ref,
                 wpa_ref, wpb_ref, wo_ref, wq_ref, x1_ref, h2_ref, q_ref):
    oa = jnp.concatenate([oa_ref[h] for h in range(N_HEADS_FOX)], axis=1)
    ob = jnp.concatenate([ob_ref[h] for h in range(N_HEADS_DIFF)], axis=1)
    ya = jnp.dot(oa, wpa_ref[...], preferred_element_type=F32)
    yb = jnp.dot(ob, wpb_ref[...], preferred_element_type=F32)
    mix_in = sga_ref[...].astype(F32) * ya + sgb_ref[...].astype(F32) * yb
    mix = jnp.dot(mix_in.astype(BF16), wo_ref[...], preferred_element_type=F32)
    x1 = x_ref[...] + gta_ref[...] * mix
    x1_ref[...] = x1
    h2 = _modnorm(x1, g_ref[...], scm_ref[...], shm_ref[...])
    h2_ref[...] = h2.astype(BF16)
    q_ref[...] = jnp.dot(h2, wq_ref[...], precision=HIGHEST, preferred_element_type=F32)


def _post_call(oa, ob, sga, sgb, x2d, gta, shm, scm, g, w_pa, w_pb, w_o, w_q, *, tb, per_block_mod):
    n = x2d.shape[0]
    nb = n // tb
    mod_map = (lambda i: (i, 0, 0)) if per_block_mod else (lambda i: (0, 0, 0))
    const2 = lambda i: (0, 0)
    row = lambda i: (i, 0)
    return pl.pallas_call(
        _post_kernel,
        out_shape=(
            jax.ShapeDtypeStruct((n, D_MODEL), F32),
            jax.ShapeDtypeStruct((n, D_MODEL), BF16),
            jax.ShapeDtypeStruct((n, D_MODEL), F32),
        ),
        grid=(nb,),
        in_specs=[
            pl.BlockSpec((N_HEADS_FOX, tb, LANES), lambda i: (0, i, 0)),
            pl.BlockSpec((N_HEADS_DIFF, tb, LANES), lambda i: (0, i, 0)),
            pl.BlockSpec((tb, D_MODEL), row),
            pl.BlockSpec((tb, D_MODEL), row),
            pl.BlockSpec((tb, D_MODEL), row),
            pl.BlockSpec((None, 1, D_MODEL), mod_map),
            pl.BlockSpec((None, 1, D_MODEL), mod_map),
            pl.BlockSpec((None, 1, D_MODEL), mod_map),
            pl.BlockSpec((1, D_MODEL), const2),
            pl.BlockSpec(w_pa.shape, const2),
            pl.BlockSpec(w_pb.shape, const2),
            pl.BlockSpec(w_o.shape, const2),
            pl.BlockSpec(w_q.shape, const2),
        ],
        out_specs=(
            pl.BlockSpec((tb, D_MODEL), row),
            pl.BlockSpec((tb, D_MODEL), row),
            pl.BlockSpec((tb, D_MODEL), row),
        ),
        compiler_params=_cparams(("arbitrary",)),
        name="post_attn",
    )(oa, ob, sga, sgb, x2d, gta, shm, scm, g, w_pa, w_pb, w_o, w_q)


def _extract_top(x, n_keep, out_ref, slot):
    rows = x.shape[0]
    rid = lax.broadcasted_iota(jnp.int32, x.shape, 0)

    def body(t, xc):
        m = jnp.max(xc, axis=0, keepdims=True)
        out_ref[slot, pl.ds(t, 1), :] = m
        first = jnp.min(jnp.where(xc == m, rid, rows), axis=0, keepdims=True)
        return jnp.where(rid == first, -jnp.inf, xc)

    lax.fori_loop(0, n_keep, body, x)


def _peer_router(h2_ref, q_ref, sk_ref, h2t_ref, qt_ref, sraw_ref, thr_ref, e1_ref, e2_ref, vbuf_ref, scbuf_ref):
    tb = h2_ref.shape[0]
    h2t_ref[...] = h2_ref[...].astype(F32).T.astype(BF16)
    qt_ref[...] = q_ref[...].T
    vbuf_ref[...] = jnp.full(vbuf_ref.shape, -jnp.inf, F32)
    scbuf_ref[...] = jnp.full(scbuf_ref.shape, -jnp.inf, F32)

    def half_body(hp, c):
        qs = qt_ref[pl.ds(pl.multiple_of(hp * PEER_KEY_HALF, PEER_KEY_HALF), PEER_KEY_HALF), :]
        s = jnp.dot(sk_ref[hp], qs, precision=HIGHEST, preferred_element_type=F32)
        sraw_ref[hp] = s
        _extract_top(s, N_EXTRACT, vbuf_ref, hp)
        return c

    lax.fori_loop(0, 2 * PEER_HEADS, half_body, 0)

    rid24 = lax.broadcasted_iota(jnp.int32, (VBUF_ROWS, tb), 0)
    rid8 = lax.broadcasted_iota(jnp.int32, (SUBLANES, tb), 0)

    def head_body(h, c):
        v1 = vbuf_ref[2 * h]
        v2 = vbuf_ref[2 * h + 1]
        pieces = [jnp.where(rid24 < N_EXTRACT, v1[0:1, :] + v2, -jnp.inf)]
        for a in range(1, N_EXTRACT):
            nb = N_EXTRACT // (a + 1)
            pieces.append(jnp.where(rid8 < nb, v1[a:a + 1, :] + v2[0:SUBLANES, :], -jnp.inf))
        cand = jnp.concatenate(pieces, axis=0)
        _extract_top(cand, N_EXTRACT, scbuf_ref, h)
        sc = scbuf_ref[h]
        top = sc[0:1, :]
        z = jnp.sum(jnp.where(rid24 < PEER_TOPK, jnp.exp(sc - top), 0.0), axis=0, keepdims=True)
        thr = 0.5 * (sc[PEER_TOPK - 1:PEER_TOPK, :] + sc[PEER_TOPK:PEER_TOPK + 1, :])
        s1 = sraw_ref[2 * h]
        s2 = sraw_ref[2 * h + 1]
        thr_ref[h] = thr - s1
        e1_ref[h] = jnp.exp(s1 - v1[0:1, :]) * (1.0 / z)
        e2_ref[h] = jnp.exp(s2 - v2[0:1, :])
        return c

    lax.fori_loop(0, PEER_HEADS, head_body, 0)


def _peer_kernel(h2_ref, q_ref, sk_ref, u_ref, vt_ref, o_ref,
                 h2t_ref, qt_ref, sraw_ref, thr_ref, e1_ref, e2_ref, vbuf_ref, scbuf_ref, acc_ref, *, eb):
    e = pl.program_id(1)
    ne = pl.num_programs(1)

    @pl.when(e == 0)
    def _():
        _peer_router(h2_ref, q_ref, sk_ref, h2t_ref, qt_ref, sraw_ref, thr_ref, e1_ref, e2_ref, vbuf_ref, scbuf_ref)
        acc_ref[...] = jnp.zeros(acc_ref.shape, F32)

    st = jnp.dot(u_ref[...], h2t_ref[...], preferred_element_type=F32)
    n_first = eb // PEER_N_KEYS
    coeffs = []
    for kf in range(n_first):
        i1 = e * n_first + kf
        s_blk = st[kf * PEER_N_KEYS:(kf + 1) * PEER_N_KEYS, :]
        gate = jnp.zeros(s_blk.shape, F32)
        for h in range(PEER_HEADS):
            thr_row = thr_ref[h, pl.ds(i1, 1), :]
            e1_row = e1_ref[h, pl.ds(i1, 1), :]
            gate = gate + jnp.where(sraw_ref[2 * h + 1] > thr_row, e2_ref[h] * e1_row, 0.0)
        act = 0.5 * s_blk * (1.0 + lax.erf(s_blk * (2.0 ** -0.5)))
        coeffs.append((act * gate).astype(BF16))
    coeff = jnp.concatenate(coeffs, axis=0)
    acc_ref[...] += jnp.dot(vt_ref[...], coeff, preferred_element_type=F32)

    @pl.when(e == ne - 1)
    def _():
        o_ref[...] = acc_ref[...].T


def _peer_call(h2, q, subkeys, u_b, vt_b, *, tb, eb):
    n = h2.shape[0]
    n_exp = u_b.shape[0]
    nt = n // tb
    ne = n_exp // eb
    return pl.pallas_call(
        functools.partial(_peer_kernel, eb=eb),
        out_shape=jax.ShapeDtypeStruct((n, D_MODEL), F32),
        grid=(nt, ne),
        in_specs=[
            pl.BlockSpec((tb, D_MODEL), lambda t, e: (t, 0)),
            pl.BlockSpec((tb, D_MODEL), lambda t, e: (t, 0)),
            pl.BlockSpec(subkeys.shape, lambda t, e: (0, 0, 0)),
            pl.BlockSpec((eb, D_MODEL), lambda t, e: (e, 0)),
            pl.BlockSpec((D_MODEL, eb), lambda t, e: (0, e)),
        ],
        out_specs=pl.BlockSpec((tb, D_MODEL), lambda t, e: (t, 0)),
        scratch_shapes=[
            pltpu.VMEM((D_MODEL, tb), BF16),
            pltpu.VMEM((D_MODEL, tb), F32),
            pltpu.VMEM((2 * PEER_HEADS, PEER_N_KEYS, tb), F32),
            pltpu.VMEM((PEER_HEADS, PEER_N_KEYS, tb), F32),
            pltpu.VMEM((PEER_HEADS, PEER_N_KEYS, tb), F32),
            pltpu.VMEM((PEER_HEADS, PEER_N_KEYS, tb), F32),
            pltpu.VMEM((2 * PEER_HEADS, VBUF_ROWS, tb), F32),
            pltpu.VMEM((PEER_HEADS, VBUF_ROWS, tb), F32),
            pltpu.VMEM((D_MODEL, tb), F32),
        ],
        compiler_params=_cparams(("arbitrary", "arbitrary")),
        name="peer",
    )(h2, q, subkeys, u_b, vt_b)


def _final_kernel(x1_ref, peer_ref, gtm_ref, g_ref, y_ref):
    x2 = x1_ref[...] + gtm_ref[...] * peer_ref[...]
    ms = jnp.mean(x2 * x2, axis=-1, keepdims=True)
    y_ref[...] = x2 * lax.rsqrt(ms + NORM_EPS) * g_ref[...]


def _final_call(x1, peer_all, gtm, g_final, *, tb, block_off, per_block_mod):
    n = x1.shape[0]
    nb = n // tb
    mod_map = (lambda i: (i, 0, 0)) if per_block_mod else (lambda i: (0, 0, 0))
    return pl.pallas_call(
        _final_kernel,
        out_shape=jax.ShapeDtypeStruct((n, D_MODEL), F32),
        grid=(nb,),
        in_specs=[
            pl.BlockSpec((tb, D_MODEL), lambda i: (i, 0)),
            pl.BlockSpec((tb, D_MODEL), lambda i: (i + block_off, 0)),
            pl.BlockSpec((None, 1, D_MODEL), mod_map),
            pl.BlockSpec((1, D_MODEL), lambda i: (0, 0)),
        ],
        out_specs=pl.BlockSpec((tb, D_MODEL), lambda i: (i, 0)),
        compiler_params=_cparams(("arbitrary",)),
        name="final_norm",
    )(x1, peer_all, gtm, g_final)


def _rope_tables(pos):
    half = HEAD_DIM // 2
    inv = ROPE_THETA ** (-jnp.arange(half, dtype=F32) / half)
    ang = pos.astype(F32)[:, None] * inv[None, :]
    cos = jnp.cos(ang)
    sin = jnp.sin(ang)
    cos_t = jnp.concatenate([cos, cos, cos, cos], axis=1)
    sin_t = jnp.concatenate([-sin, sin, -sin, sin], axis=1)
    return cos_t, sin_t


def _pad_cols(w, width):
    return jnp.pad(w, ((0, 0), (0, width - w.shape[1])))


def kernel(x_prompt, x_sample, cache_fox_k, cache_fox_v, cache_fox_logf, cache_diff_k, cache_diff_v, c_prompt, c_sample, w_ada, b_ada, g_attn, w_in, b_f, diff_lq1, diff_lk1, diff_lq2, diff_lk2, diff_subln_g, w_pa, w_pb, w_o, g_ffn, peer_wq, peer_subkeys, peer_u, peer_v, g_final):
    nb_p, seq, d = x_prompt.shape
    nb_s, dec_seq, _ = x_sample.shape
    past = cache_fox_k.shape[2]
    assert nb_p == 1 and d == D_MODEL and w_ada.shape[0] == 1
    assert seq % ATTN_BLOCK == 0 and past % ATTN_BLOCK == 0 and dec_seq % CHUNK == 0 and dec_seq <= ATTN_BLOCK
    n_s = nb_s * dec_seq
    assert (seq + n_s) % PEER_TOKEN_BLOCK == 0

    w_in0 = w_in[0]
    o_fq, o_fk, o_fv = 0, FOX_WIDTH, 2 * FOX_WIDTH
    o_ff = 3 * FOX_WIDTH
    o_dq = o_ff + N_HEADS_FOX
    o_dk = o_dq + DIFF_QK_WIDTH
    o_dv = o_dk + DIFF_QK_WIDTH
    o_ga = o_dv + DIFF_V_WIDTH
    w_fox = jnp.concatenate([w_in0[:, o_fq:o_fk] * ATTN_SCALE, w_in0[:, o_fk:o_ff]], axis=1).astype(BF16)
    w_ff = _pad_cols(w_in0[:, o_ff:o_dq], LANES)
    b_f_pad = _pad_cols(b_f, LANES)
    w_dq = (w_in0[:, o_dq:o_dk] * ATTN_SCALE).reshape(D_MODEL, N_HEADS_DIFF, 2, HEAD_DIM)
    zeros_q = jnp.zeros((D_MODEL, N_HEADS_DIFF, HEAD_DIM), F32)
    w_dq_pad = jnp.stack([
        jnp.concatenate([w_dq[:, :, 0], zeros_q], axis=-1),
        jnp.concatenate([zeros_q, w_dq[:, :, 1]], axis=-1)], axis=2).reshape(D_MODEL, 2 * DIFF_QK_WIDTH)
    w_dg = jnp.concatenate([w_dq_pad, w_in0[:, o_dk:]], axis=1).astype(BF16)
    w_pa_pad = jnp.pad(w_pa[0].reshape(N_HEADS_FOX, HEAD_DIM, D_MODEL),
                       ((0, 0), (0, LANES - HEAD_DIM), (0, 0))).reshape(N_HEADS_FOX * LANES, D_MODEL).astype(BF16)
    w_pb_b = w_pb[0].astype(BF16)
    w_o_b = w_o[0].astype(BF16)
    w_q = peer_wq[0]
    subkeys = peer_subkeys[0].reshape(2 * PEER_HEADS, PEER_N_KEYS, PEER_KEY_HALF)
    u_b = peer_u[0].astype(BF16)
    vt_b = peer_v[0].T.astype(BF16)
    consts = _fox_aug_consts()
    g_attn2 = g_attn
    g_ffn2 = g_ffn
    g_final2 = g_final.reshape(1, D_MODEL)

    n_rows = 1 + nb_s
    n_rows_pad = -(-n_rows // SUBLANES) * SUBLANES
    c_rows = jnp.pad(jnp.concatenate([c_prompt, c_sample], axis=0), ((0, n_rows_pad - n_rows), (0, 0)))
    mod = _ada_call(c_rows, w_ada[0], b_ada).reshape(n_rows_pad, 6, 1, D_MODEL)
    mod_p = [mod[0:1, k] for k in range(6)]
    mod_s = [mod[1:n_rows, k] for k in range(6)]

    xp = x_prompt.reshape(seq, D_MODEL)
    xs = x_sample.reshape(n_s, D_MODEL)

    fk_p, fv_p, logf_p, qa_p, ka_p, va_p, base_p = _fox_pre_call(
        xp, mod_p[0], mod_p[1], g_attn2, w_fox, w_ff, b_f_pad, consts,
        tb=TOKEN_BLOCK, per_block_mod=False, carry=True)
    fk_s, fv_s, logf_s, qa_s, ka_s, va_s, _ = _fox_pre_call(
        xs, mod_s[0], mod_s[1], g_attn2, w_fox, w_ff, b_f_pad, consts,
        tb=dec_seq, per_block_mod=True, carry=False)
    cos_p, sin_p = _rope_tables(jnp.arange(seq, dtype=jnp.int32))
    cos_s1, sin_s1 = _rope_tables(jnp.arange(past, past + dec_seq, dtype=jnp.int32))
    cos_s = jnp.tile(cos_s1, (nb_s, 1))
    sin_s = jnp.tile(sin_s1, (nb_s, 1))
    dk_p, dv_p, dqp_p, dkb_p, dva_p, sga_p, sgb_p = _diff_pre_call(
        xp, mod_p[0], mod_p[1], g_attn2, w_dg, cos_p, sin_p, tb=TOKEN_BLOCK, per_block_mod=False)
    dk_s, dv_s, dqp_s, dkb_s, dva_s, sga_s, sgb_s = _diff_pre_call(
        xs, mod_s[0], mod_s[1], g_attn2, w_dg, cos_s, sin_s, tb=dec_seq, per_block_mod=True)

    nblk = seq // ATTN_BLOCK
    base_hp = base_p.reshape(nblk, LANES)[:, :N_HEADS_FOX].T.reshape(-1)
    oa_p = _fox_attn_call(qa_p[None], ka_p[None], va_p[None], base_hp, base_hp,
                          tq=ATTN_BLOCK, tk=ATTN_BLOCK, qoff=0)[0]

    ka_c, va_c, base_c, tot_c = _fox_cache_call(
        cache_fox_k[0].reshape(nb_s, past, FOX_WIDTH), cache_fox_v[0].reshape(nb_s, past, FOX_WIDTH),
        jnp.pad(cache_fox_logf[0], ((0, 0), (0, 0), (0, LANES - N_HEADS_FOX))), consts)
    kv_len = past + ATTN_BLOCK
    pad_rows = kv_len - past - dec_seq

    def per_stream(a):
        return a.reshape(a.shape[0], nb_s, dec_seq, a.shape[2]).transpose(1, 0, 2, 3)

    def with_new(cache_part, new_part):
        z = jnp.zeros(new_part.shape[:2] + (pad_rows, new_part.shape[3]), new_part.dtype)
        return jnp.concatenate([cache_part, new_part, z], axis=2)

    k_all = with_new(ka_c, per_stream(ka_s))
    v_all = with_new(va_c, per_stream(va_s))
    nr = past // ATTN_BLOCK
    base_c2 = base_c.reshape(nb_s, nr, LANES)[:, :, :N_HEADS_FOX]
    tot_c2 = tot_c.reshape(nb_s, nr, LANES)[:, :, :N_HEADS_FOX]
    base_new = base_c2[:, nr - 1:nr] + tot_c2[:, nr - 1:nr]
    kbase_s = jnp.concatenate([base_c2, base_new], axis=1).transpose(0, 2, 1)
    qbase_s = kbase_s[:, :, nr:]
    oa_s = _fox_attn_call(per_stream(qa_s), k_all, v_all, qbase_s.reshape(-1), kbase_s.reshape(-1),
                          tq=dec_seq, tk=ATTN_BLOCK, qoff=past)
    oa_s = oa_s.transpose(1, 0, 2, 3).reshape(N_HEADS_FOX, n_s, LANES)

    lqk = (diff_lq1, diff_lk1, diff_lq2, diff_lk2, diff_subln_g)
    ob_p = _diff_attn_call(dqp_p[None], dkb_p[None], dva_p[None], *lqk,
                           tq=ATTN_BLOCK, tk=ATTN_BLOCK, qoff=0)[0]
    cdk = cache_diff_k[0].reshape(nb_s, past, N_HEADS_DIFF, LANES).transpose(0, 2, 1, 3).astype(BF16)
    cdv = cache_diff_v[0].transpose(0, 2, 1, 3).astype(BF16)
    ones_col = jnp.zeros((nb_s, N_HEADS_DIFF, past, LANES), BF16).at[..., 0].set(1.0)
    cdv_aug = jnp.concatenate([cdv, ones_col], axis=-1)
    dk_all = with_new(cdk, per_stream(dkb_s))
    dv_all = with_new(cdv_aug, per_stream(dva_s))
    ob_s = _diff_attn_call(per_stream(dqp_s), dk_all, dv_all, *lqk,
                           tq=dec_seq, tk=ATTN_BLOCK, qoff=past)
    ob_s = ob_s.transpose(1, 0, 2, 3).reshape(N_HEADS_DIFF, n_s, LANES)

    x1_p, h2_p, q_p = _post_call(oa_p, ob_p, sga_p, sgb_p, xp, mod_p[2], mod_p[3], mod_p[4], g_ffn2,
                                 w_pa_pad, w_pb_b, w_o_b, w_q, tb=TOKEN_BLOCK, per_block_mod=False)
    x1_s, h2_s, q_s = _post_call(oa_s, ob_s, sga_s, sgb_s, xs, mod_s[2], mod_s[3], mod_s[4], g_ffn2,
                                 w_pa_pad, w_pb_b, w_o_b, w_q, tb=dec_seq, per_block_mod=True)

    h2_all = jnp.concatenate([h2_p, h2_s], axis=0)
    q_all = jnp.concatenate([q_p, q_s], axis=0)
    peer_all = _peer_call(h2_all, q_all, subkeys, u_b, vt_b, tb=PEER_TOKEN_BLOCK, eb=PEER_EXPERT_BLOCK)

    y_p = _final_call(x1_p, peer_all, mod_p[5], g_final2, tb=TOKEN_BLOCK, block_off=0, per_block_mod=False)
    y_s = _final_call(x1_s, peer_all, mod_s[5], g_final2, tb=dec_seq, block_off=seq // dec_seq, per_block_mod=True)

    dt = x_prompt.dtype
    return (
        y_p.reshape(1, seq, D_MODEL),
        y_s.reshape(nb_s, dec_seq, D_MODEL),
        fk_p.reshape(1, 1, seq, N_HEADS_FOX, HEAD_DIM),
        fv_p.reshape(1, 1, seq, N_HEADS_FOX, HEAD_DIM),
        logf_p.astype(dt).reshape(1, 1, seq, N_HEADS_FOX),
        dk_p.reshape(1, 1, seq, N_HEADS_DIFF, 2, HEAD_DIM),
        dv_p.reshape(1, 1, seq, N_HEADS_DIFF, DIFF_V_DIM),
        fk_s.reshape(1, nb_s, dec_seq, N_HEADS_FOX, HEAD_DIM),
        fv_s.reshape(1, nb_s, dec_seq, N_HEADS_FOX, HEAD_DIM),
        logf_s.astype(dt).reshape(1, nb_s, dec_seq, N_HEADS_FOX),
        dk_s.reshape(1, nb_s, dec_seq, N_HEADS_DIFF, 2, HEAD_DIM),
        dv_s.reshape(1, nb_s, dec_seq, N_HEADS_DIFF, DIFF_V_DIM),
    )
```

```python
import functools
import math

import numpy as np
import jax
import jax.numpy as jnp
from jax import lax
from jax.experimental import pallas as pl
from jax.experimental.pallas import tpu as pltpu

F32 = jnp.float32
BF16 = jnp.bfloat16
HIGHEST = lax.Precision.HIGHEST

D_MODEL = 1024
CHUNK = 64
HEAD_DIM = 64
N_HEADS_FOX = 8
N_HEADS_DIFF = 4
FOX_WIDTH = N_HEADS_FOX * HEAD_DIM
DIFF_QK_WIDTH = N_HEADS_DIFF * 2 * HEAD_DIM
DIFF_V_DIM = 2 * HEAD_DIM
DIFF_V_WIDTH = N_HEADS_DIFF * DIFF_V_DIM
ROPE_THETA = 10000.0
NORM_EPS = 1e-6
SUBLN_EPS = 1e-5
PEER_HEADS = 8
PEER_N_KEYS = 128
PEER_KEY_HALF = 64
PEER_TOPK = 16
LAM_INIT = 0.8 - 0.6 * math.exp(-0.3 * 0)
ATTN_SCALE = HEAD_DIM ** -0.5

LANES = 128
SUBLANES = 8
VMEM_LIMIT_BYTES = 56 * 1024 * 1024

ATTN_BLOCK = 512
TOKEN_BLOCK = 512
PEER_TOKEN_BLOCK = 512
PEER_EXPERT_BLOCK = 2048
PEER_SUB_BLOCK = 512
NOT_RANKED = 64.0

Q_BIAS_COL = HEAD_DIM
Q_ONE_COL = HEAD_DIM + 3
V_ONE_COL = HEAD_DIM


def _cparams(sem):
    return pltpu.CompilerParams(dimension_semantics=sem, vmem_limit_bytes=VMEM_LIMIT_BYTES)


def _log_sigmoid(x):
    return jnp.minimum(x, 0.0) - jnp.log1p(jnp.exp(-jnp.abs(x)))


def _modnorm(x, g, sc, sh):
    ms = jnp.mean(x * x, axis=-1, keepdims=True)
    return (x * lax.rsqrt(ms + NORM_EPS) * g) * (1.0 + sc) + sh


def _split3(a):
    hi = a.astype(BF16)
    r = a - hi.astype(F32)
    mid = r.astype(BF16)
    lo = (r - mid.astype(F32)).astype(BF16)
    return jnp.concatenate([hi, mid, lo], axis=1)


def _incl_cumsum_rows(v):
    n = v.shape[0]
    row = lax.broadcasted_iota(jnp.int32, (n, n), 0)
    col = lax.broadcasted_iota(jnp.int32, (n, n), 1)
    tri = (row >= col).astype(F32)
    return jnp.dot(tri, v, precision=HIGHEST, preferred_element_type=F32)


def _ada_kernel(c_ref, w_ref, b_ref, o_ref):
    c = c_ref[...]
    s = c * jax.nn.sigmoid(c)
    o_ref[...] = jnp.dot(s, w_ref[...], precision=HIGHEST, preferred_element_type=F32) + b_ref[...]


def _ada_call(c_rows, w_ada, b_ada):
    n = c_rows.shape[0]
    return pl.pallas_call(
        _ada_kernel,
        out_shape=jax.ShapeDtypeStruct((n, 6 * D_MODEL), F32),
        grid=(6,),
        in_specs=[
            pl.BlockSpec((n, D_MODEL), lambda j: (0, 0)),
            pl.BlockSpec((D_MODEL, D_MODEL), lambda j: (0, j)),
            pl.BlockSpec((1, D_MODEL), lambda j: (0, j)),
        ],
        out_specs=pl.BlockSpec((n, D_MODEL), lambda j: (0, j)),
        compiler_params=_cparams(("arbitrary",)),
        name="ada_mod",
    )(c_rows, w_ada, b_ada)


def _fox_aug_consts():
    place = np.zeros((FOX_WIDTH, N_HEADS_FOX * LANES), np.float32)
    for h in range(N_HEADS_FOX):
        for d in range(HEAD_DIM):
            place[h * HEAD_DIM + d, h * LANES + d] = 1.0
    sel_q = np.zeros((3 * LANES, N_HEADS_FOX * LANES), np.float32)
    sel_k = np.zeros((3 * LANES, N_HEADS_FOX * LANES), np.float32)
    one_q = np.zeros((1, N_HEADS_FOX * LANES), np.float32)
    one_k = np.zeros((1, N_HEADS_FOX * LANES), np.float32)
    one_v = np.zeros((1, N_HEADS_FOX * LANES), np.float32)
    for h in range(N_HEADS_FOX):
        for piece in range(3):
            sel_q[piece * LANES + h, h * LANES + Q_BIAS_COL + piece] = 1.0
            sel_k[piece * LANES + h, h * LANES + Q_ONE_COL + piece] = -1.0
            one_q[0, h * LANES + Q_ONE_COL + piece] = 1.0
            one_k[0, h * LANES + Q_BIAS_COL + piece] = 1.0
        one_v[0, h * LANES + V_ONE_COL] = 1.0
    return (jnp.asarray(place, BF16), jnp.asarray(sel_q, BF16), jnp.asarray(sel_k, BF16),
            jnp.asarray(one_q), jnp.asarray(one_k), jnp.asarray(one_v))


def _store_heads(ref, wide, n_heads, width):
    for h in range(n_heads):
        ref[h] = wide[:, h * width:(h + 1) * width]


def _fox_pre_kernel(x_ref, sh_ref, sc_ref, g_ref, w_ref, wff_ref, bf_ref, place_ref, selq_ref, selk_ref,
                    oneq_ref, onek_ref, onev_ref,
                    fk_ref, fv_ref, logf_ref, qa_ref, ka_ref, va_ref, base_ref,
                    carry_ref, *, carry):
    if carry:
        @pl.when(pl.program_id(0) == 0)
        def _():
            carry_ref[...] = jnp.zeros(carry_ref.shape, F32)
    else:
        carry_ref[...] = jnp.zeros(carry_ref.shape, F32)

    h = _modnorm(x_ref[...], g_ref[...], sc_ref[...], sh_ref[...])
    proj = jnp.dot(h.astype(BF16), w_ref[...], preferred_element_type=F32)
    fq = proj[:, :FOX_WIDTH]
    fk = proj[:, FOX_WIDTH:2 * FOX_WIDTH]
    fv = proj[:, 2 * FOX_WIDTH:]
    fk_ref[...] = fk
    fv_ref[...] = fv

    ff = jnp.dot(h, wff_ref[...], precision=HIGHEST, preferred_element_type=F32) + bf_ref[...]
    lane = lax.broadcasted_iota(jnp.int32, ff.shape, 1)
    logf = jnp.where(lane < N_HEADS_FOX, _log_sigmoid(ff), 0.0)
    logf_ref[...] = logf[:, :N_HEADS_FOX]

    a = _incl_cumsum_rows(logf)
    a3 = _split3(a)
    place = place_ref[...]
    qa = (jnp.dot(fq.astype(BF16), place, preferred_element_type=F32)
          + jnp.dot(a3, selq_ref[...], preferred_element_type=F32) + oneq_ref[...])
    ka = (jnp.dot(fk.astype(BF16), place, preferred_element_type=F32)
          + jnp.dot(a3, selk_ref[...], preferred_element_type=F32) + onek_ref[...])
    va = jnp.dot(fv.astype(BF16), place, preferred_element_type=F32) + onev_ref[...]
    _store_heads(qa_ref, qa.astype(BF16), N_HEADS_FOX, LANES)
    _store_heads(ka_ref, ka.astype(BF16), N_HEADS_FOX, LANES)
    _store_heads(va_ref, va.astype(BF16), N_HEADS_FOX, LANES)

    base_ref[...] = carry_ref[...]
    carry_ref[...] = carry_ref[...] + a[a.shape[0] - 1:, :]


def _fox_pre_call(x2d, sh, sc, g, w_fox, w_ff, b_f, consts, *, tb, per_block_mod, carry):
    n = x2d.shape[0]
    nb = n // tb
    place, sel_q, sel_k, one_q, one_k, one_v = consts
    mod_map = (lambda i: (i, 0, 0)) if per_block_mod else (lambda i: (0, 0, 0))
    const2 = lambda i: (0, 0)
    hm = N_HEADS_FOX
    return pl.pallas_call(
        functools.partial(_fox_pre_kernel, carry=carry),
        out_shape=(
            jax.ShapeDtypeStruct((n, FOX_WIDTH), F32),
            jax.ShapeDtypeStruct((n, FOX_WIDTH), F32),
            jax.ShapeDtypeStruct((n, N_HEADS_FOX), F32),
            jax.ShapeDtypeStruct((hm, n, LANES), BF16),
            jax.ShapeDtypeStruct((hm, n, LANES), BF16),
            jax.ShapeDtypeStruct((hm, n, LANES), BF16),
            jax.ShapeDtypeStruct((nb, 1, LANES), F32),
        ),
        grid=(nb,),
        in_specs=[
            pl.BlockSpec((tb, D_MODEL), lambda i: (i, 0)),
            pl.BlockSpec((None, 1, D_MODEL), mod_map),
            pl.BlockSpec((None, 1, D_MODEL), mod_map),
            pl.BlockSpec((1, D_MODEL), const2),
            pl.BlockSpec(w_fox.shape, const2),
            pl.BlockSpec(w_ff.shape, const2),
            pl.BlockSpec(b_f.shape, const2),
            pl.BlockSpec(place.shape, const2),
            pl.BlockSpec(sel_q.shape, const2),
            pl.BlockSpec(sel_k.shape, const2),
            pl.BlockSpec(one_q.shape, const2),
            pl.BlockSpec(one_k.shape, const2),
            pl.BlockSpec(one_v.shape, const2),
        ],
        out_specs=(
            pl.BlockSpec((tb, FOX_WIDTH), lambda i: (i, 0)),
            pl.BlockSpec((tb, FOX_WIDTH), lambda i: (i, 0)),
            pl.BlockSpec((tb, N_HEADS_FOX), lambda i: (i, 0)),
            pl.BlockSpec((hm, tb, LANES), lambda i: (0, i, 0)),
            pl.BlockSpec((hm, tb, LANES), lambda i: (0, i, 0)),
            pl.BlockSpec((hm, tb, LANES), lambda i: (0, i, 0)),
            pl.BlockSpec((None, 1, LANES), lambda i: (i, 0, 0)),
        ),
        scratch_shapes=[pltpu.VMEM((1, LANES), F32)],
        compiler_params=_cparams(("arbitrary",)),
        name="fox_pre",
    )(x2d, sh, sc, g, w_fox, w_ff, b_f, place, sel_q, sel_k, one_q, one_k, one_v)


def _fox_cache_kernel(ck_ref, cv_ref, clf_ref, place_ref, selk_ref, onek_ref, onev_ref,
                      ka_ref, va_ref, base_ref, tot_ref, carry_ref):
    @pl.when(pl.program_id(1) == 0)
    def _():
        carry_ref[...] = jnp.zeros(carry_ref.shape, F32)

    a = _incl_cumsum_rows(clf_ref[...])
    a3 = _split3(a)
    place = place_ref[...]
    ka = (jnp.dot(ck_ref[...].astype(BF16), place, preferred_element_type=F32)
          + jnp.dot(a3, selk_ref[...], preferred_element_type=F32) + onek_ref[...])
    va = jnp.dot(cv_ref[...].astype(BF16), place, preferred_element_type=F32) + onev_ref[...]
    _store_heads(ka_ref, ka.astype(BF16), N_HEADS_FOX, LANES)
    _store_heads(va_ref, va.astype(BF16), N_HEADS_FOX, LANES)
    tot = a[a.shape[0] - 1:, :]
    base_ref[...] = carry_ref[...]
    tot_ref[...] = tot
    carry_ref[...] = carry_ref[...] + tot


def _fox_cache_call(ck, cv, clf_pad, consts):
    nbatch, plen, _ = ck.shape
    tb = ATTN_BLOCK
    nr = plen // tb
    place, _, sel_k, _, one_k, one_v = consts
    const2 = lambda b, r: (0, 0)
    hm = N_HEADS_FOX
    return pl.pallas_call(
        _fox_cache_kernel,
        out_shape=(
            jax.ShapeDtypeStruct((nbatch, hm, plen, LANES), BF16),
            jax.ShapeDtypeStruct((nbatch, hm, plen, LANES), BF16),
            jax.ShapeDtypeStruct((nbatch, nr, 1, LANES), F32),
            jax.ShapeDtypeStruct((nbatch, nr, 1, LANES), F32),
        ),
        grid=(nbatch, nr),
        in_specs=[
            pl.BlockSpec((None, tb, FOX_WIDTH), lambda b, r: (b, r, 0)),
            pl.BlockSpec((None, tb, FOX_WIDTH), lambda b, r: (b, r, 0)),
            pl.BlockSpec((None, tb, LANES), lambda b, r: (b, r, 0)),
            pl.BlockSpec(place.shape, const2),
            pl.BlockSpec(sel_k.shape, const2),
            pl.BlockSpec(one_k.shape, const2),
            pl.BlockSpec(one_v.shape, const2),
        ],
        out_specs=(
            pl.BlockSpec((None, hm, tb, LANES), lambda b, r: (b, 0, r, 0)),
            pl.BlockSpec((None, hm, tb, LANES), lambda b, r: (b, 0, r, 0)),
            pl.BlockSpec((None, None, 1, LANES), lambda b, r: (b, r, 0, 0)),
            pl.BlockSpec((None, None, 1, LANES), lambda b, r: (b, r, 0, 0)),
        ),
        scratch_shapes=[pltpu.VMEM((1, LANES), F32)],
        compiler_params=_cparams(("arbitrary", "arbitrary")),
        name="fox_cache_prep",
    )(ck, cv, clf_pad, place, sel_k, one_k, one_v)


def _rope_lanes(x, cos, sin_signed):
    n = x.shape[1]
    reps = n // LANES
    cos_w = jnp.tile(cos, (1, reps))
    sin_w = jnp.tile(sin_signed, (1, reps))
    lane = lax.broadcasted_iota(jnp.int32, x.shape, 1)
    first_half = (lane % HEAD_DIM) < (HEAD_DIM // 2)
    rot = jnp.where(first_half, pltpu.roll(x, n - HEAD_DIM // 2, 1), pltpu.roll(x, HEAD_DIM // 2, 1))
    return x * cos_w + rot * sin_w


def _diff_pre_kernel(x_ref, sh_ref, sc_ref, g_ref, w_ref, cos_ref, sin_ref,
                     dk_ref, dv_ref, qp_ref, kb_ref, va_ref, sga_ref, sgb_ref):
    h = _modnorm(x_ref[...], g_ref[...], sc_ref[...], sh_ref[...])
    proj = jnp.dot(h.astype(BF16), w_ref[...], preferred_element_type=F32)
    qw = 2 * DIFF_QK_WIDTH
    dq_pad = proj[:, :qw]
    dk = proj[:, qw:qw + DIFF_QK_WIDTH]
    dv = proj[:, qw + DIFF_QK_WIDTH:qw + DIFF_QK_WIDTH + DIFF_V_WIDTH]
    gates = proj[:, qw + DIFF_QK_WIDTH + DIFF_V_WIDTH:]
    cos = cos_ref[...]
    sin = sin_ref[...]
    dq_rot = _rope_lanes(dq_pad, cos, sin)
    dk_rot = _rope_lanes(dk, cos, sin)
    dk_ref[...] = dk_rot
    dv_ref[...] = dv
    _store_heads(qp_ref, dq_rot.astype(BF16), 2 * N_HEADS_DIFF, LANES)
    _store_heads(kb_ref, dk_rot.astype(BF16), N_HEADS_DIFF, LANES)
    tb = dv.shape[0]
    lane = lax.broadcasted_iota(jnp.int32, (tb, LANES), 1)
    ones_blk = (lane == 0).astype(BF16)
    dvb = dv.astype(BF16)
    for hd in range(N_HEADS_DIFF):
        va_ref[hd] = jnp.concatenate([dvb[:, hd * DIFF_V_DIM:(hd + 1) * DIFF_V_DIM], ones_blk], axis=1)
    sga_ref[...] = jax.nn.sigmoid(gates[:, :D_MODEL]).astype(BF16)
    sgb_ref[...] = jax.nn.sigmoid(gates[:, D_MODEL:]).astype(BF16)


def _diff_pre_call(x2d, sh, sc, g, w_dg, cos_t, sin_t, *, tb, per_block_mod):
    n = x2d.shape[0]
    nb = n // tb
    mod_map = (lambda i: (i, 0, 0)) if per_block_mod else (lambda i: (0, 0, 0))
    const2 = lambda i: (0, 0)
    row = lambda i: (i, 0)
    return pl.pallas_call(
        _diff_pre_kernel,
        out_shape=(
            jax.ShapeDtypeStruct((n, DIFF_QK_WIDTH), F32),
            jax.ShapeDtypeStruct((n, DIFF_V_WIDTH), F32),
            jax.ShapeDtypeStruct((2 * N_HEADS_DIFF, n, LANES), BF16),
            jax.ShapeDtypeStruct((N_HEADS_DIFF, n, LANES), BF16),
            jax.ShapeDtypeStruct((N_HEADS_DIFF, n, 2 * LANES), BF16),
            jax.ShapeDtypeStruct((n, D_MODEL), BF16),
            jax.ShapeDtypeStruct((n, D_MODEL), BF16),
        ),
        grid=(nb,),
        in_specs=[
            pl.BlockSpec((tb, D_MODEL), row),
            pl.BlockSpec((None, 1, D_MODEL), mod_map),
            pl.BlockSpec((None, 1, D_MODEL), mod_map),
            pl.BlockSpec((1, D_MODEL), const2),
            pl.BlockSpec(w_dg.shape, const2),
            pl.BlockSpec((tb, LANES), row),
            pl.BlockSpec((tb, LANES), row),
        ],
        out_specs=(
            pl.BlockSpec((tb, DIFF_QK_WIDTH), row),
            pl.BlockSpec((tb, DIFF_V_WIDTH), row),
            pl.BlockSpec((2 * N_HEADS_DIFF, tb, LANES), lambda i: (0, i, 0)),
            pl.BlockSpec((N_HEADS_DIFF, tb, LANES), lambda i: (0, i, 0)),
            pl.BlockSpec((N_HEADS_DIFF, tb, 2 * LANES), lambda i: (0, i, 0)),
            pl.BlockSpec((tb, D_MODEL), row),
            pl.BlockSpec((tb, D_MODEL), row),
        ),
        compiler_params=_cparams(("arbitrary",)),
        name="diff_pre",
    )(x2d, sh, sc, g, w_dg, cos_t, sin_t)


def _attn_pairs(nq, tq, tk, qoff):
    ii, jj, last = [], [], []
    for i in range(nq):
        jmax = (qoff + i * tq + tq - 1) // tk
        for j in range(jmax + 1):
            ii.append(i)
            jj.append(j)
            last.append(1 if j == jmax else 0)
    return (jnp.asarray(ii, jnp.int32), jnp.asarray(jj, jnp.int32), jnp.asarray(last, jnp.int32))


_NT_DIMS = (((1,), (1,)), ((), ()))


def _fox_attn_kernel(ii_ref, jj_ref, last_ref, qb_ref, kb_ref, q_ref, k_ref, v_ref, o_ref, m_ref, acc_ref,
                     *, tq, tk, nq, nk, qoff):
    nh = N_HEADS_FOX
    b = pl.program_id(0)
    p = pl.program_id(1)
    i = ii_ref[p]
    j = jj_ref[p]
    q_lo = qoff + i * tq

    @pl.when(j == 0)
    def _init():
        m_ref[...] = jnp.full(m_ref.shape, -jnp.inf, F32)
        acc_ref[...] = jnp.zeros(acc_ref.shape, F32)

    def head_step(h, masked):
        d = qb_ref[(b * nh + h) * nq + i] - kb_ref[(b * nh + h) * nk + j]
        s = lax.dot_general(q_ref[h], k_ref[h], _NT_DIMS, preferred_element_type=F32)
        if masked:
            row = lax.broadcasted_iota(jnp.int32, (tq, tk), 0) + q_lo
            col = lax.broadcasted_iota(jnp.int32, (tq, tk), 1) + j * tk
            s = jnp.where(col <= row, s, -jnp.inf)
        m_prev = m_ref[h]
        m_new = jnp.maximum(m_prev, jnp.max(s, axis=1, keepdims=True) + d)
        alpha = jnp.exp(m_prev - m_new)
        pexp = jnp.exp(s - jnp.tile(m_new - d, (1, tk // LANES)))
        acc_ref[h] = alpha * acc_ref[h] + jnp.dot(pexp.astype(BF16), v_ref[h], preferred_element_type=F32)
        m_ref[h] = m_new

    def run(masked):
        for h in range(nh):
            head_step(h, masked)

    is_diag = j * tk + tk - 1 > q_lo

    @pl.when(is_diag)
    def _():
        run(True)

    @pl.when(jnp.logical_not(is_diag))
    def _():
        run(False)

    @pl.when(last_ref[p] == 1)
    def _fin():
        def body(h, c):
            acc = acc_ref[h]
            o_ref[h] = (acc * (1.0 / acc[:, V_ONE_COL:V_ONE_COL + 1])).astype(BF16)
            return c
        lax.fori_loop(0, nh, body, 0)


def _fox_attn_call(q, k, v, qbase, kbase, *, tq, tk, qoff):
    nbatch, nh, lq, _ = q.shape
    lk = k.shape[2]
    nq = lq // tq
    nk = lk // tk
    ii, jj, last = _attn_pairs(nq, tq, tk, qoff)
    npairs = ii.shape[0]
    grid_spec = pltpu.PrefetchScalarGridSpec(
        num_scalar_prefetch=5,
        grid=(nbatch, npairs),
        in_specs=[
            pl.BlockSpec((None, nh, tq, LANES), lambda b, p, ii, jj, la, qb, kb: (b, 0, ii[p], 0)),
            pl.BlockSpec((None, nh, tk, LANES), lambda b, p, ii, jj, la, qb, kb: (b, 0, jj[p], 0)),
            pl.BlockSpec((None, nh, tk, LANES), lambda b, p, ii, jj, la, qb, kb: (b, 0, jj[p], 0)),
        ],
        out_specs=pl.BlockSpec((None, nh, tq, LANES), lambda b, p, ii, jj, la, qb, kb: (b, 0, ii[p], 0)),
        scratch_shapes=[pltpu.VMEM((nh, tq, LANES), F32), pltpu.VMEM((nh, tq, LANES), F32)],
    )
    return pl.pallas_call(
        functools.partial(_fox_attn_kernel, tq=tq, tk=tk, nq=nq, nk=nk, qoff=qoff),
        out_shape=jax.ShapeDtypeStruct((nbatch, nh, lq, LANES), BF16),
        grid_spec=grid_spec,
        compiler_params=_cparams(("arbitrary", "arbitrary")),
        name="fox_attn",
    )(ii, jj, last, qbase, kbase, q, k, v)


def _diff_attn_kernel(ii_ref, jj_ref, last_ref, q_ref, k_ref, v_ref, lq1_ref, lk1_ref, lq2_ref, lk2_ref, g_ref,
                      o_ref, m_ref, acc_ref, *, tq, tk, qoff):
    nh = N_HEADS_DIFF
    p = pl.program_id(1)
    i = ii_ref[p]
    j = jj_ref[p]
    q_lo = qoff + i * tq

    @pl.when(j == 0)
    def _init():
        m_ref[...] = jnp.full(m_ref.shape, -jnp.inf, F32)
        acc_ref[...] = jnp.zeros(acc_ref.shape, F32)

    def head_step(h, masked):
        kk = k_ref[h]
        vv = v_ref[h]
        if masked:
            row = (lax.broadcasted_iota(jnp.int32, (tq, tk), 0) + q_lo) // CHUNK
            col = (lax.broadcasted_iota(jnp.int32, (tq, tk), 1) + j * tk) // CHUNK
            visible = col <= row
        for c in range(2):
            hc = 2 * h + c
            s = lax.dot_general(q_ref[hc], kk, _NT_DIMS, preferred_element_type=F32)
            if masked:
                s = jnp.where(visible, s, -jnp.inf)
            m_prev = m_ref[hc]
            m_new = jnp.maximum(m_prev, jnp.max(s, axis=1, keepdims=True))
            alpha = jnp.exp(m_prev - m_new)
            pexp = jnp.exp(s - jnp.tile(m_new, (1, tk // LANES)))
            acc_ref[hc] = (jnp.tile(alpha, (1, 2)) * acc_ref[hc]
                           + jnp.dot(pexp.astype(BF16), vv, preferred_element_type=F32))
            m_ref[hc] = m_new

    def run(masked):
        for h in range(nh):
            head_step(h, masked)

    is_diag = (j * tk + tk - 1) // CHUNK > q_lo // CHUNK

    @pl.when(is_diag)
    def _():
        run(True)

    @pl.when(jnp.logical_not(is_diag))
    def _():
        run(False)

    @pl.when(last_ref[p] == 1)
    def _fin():
        lam = (jnp.exp(jnp.sum(lq1_ref[...] * lk1_ref[...], axis=1, keepdims=True))
               - jnp.exp(jnp.sum(lq2_ref[...] * lk2_ref[...], axis=1, keepdims=True)) + LAM_INIT)
        g = g_ref[...] * (1.0 - LAM_INIT)

        def body(h, c):
            a1 = acc_ref[2 * h]
            a2 = acc_ref[2 * h + 1]
            o1 = a1[:, :DIFF_V_DIM] * (1.0 / a1[:, DIFF_V_DIM:DIFF_V_DIM + 1])
            o2 = a2[:, :DIFF_V_DIM] * (1.0 / a2[:, DIFF_V_DIM:DIFF_V_DIM + 1])
            o = o1 - lam * o2
            ms = jnp.mean(o * o, axis=-1, keepdims=True)
            o_ref[h] = (o * lax.rsqrt(ms + SUBLN_EPS) * g).astype(BF16)
            return c
        lax.fori_loop(0, nh, body, 0)


def _diff_attn_call(q, k, v, lq1, lk1, lq2, lk2, subln_g, *, tq, tk, qoff):
    nbatch, _, lq, _ = q.shape
    lk = k.shape[2]
    nq = lq // tq
    nh = N_HEADS_DIFF
    ii, jj, last = _attn_pairs(nq, tq, tk, qoff)
    npairs = ii.shape[0]
    small = lambda b, p, ii, jj, la: (0, 0)
    grid_spec = pltpu.PrefetchScalarGridSpec(
        num_scalar_prefetch=3,
        grid=(nbatch, npairs),
        in_specs=[
            pl.BlockSpec((None, 2 * nh, tq, LANES), lambda b, p, ii, jj, la: (b, 0, ii[p], 0)),
            pl.BlockSpec((None, nh, tk, LANES), lambda b, p, ii, jj, la: (b, 0, jj[p], 0)),
            pl.BlockSpec((None, nh, tk, 2 * LANES), lambda b, p, ii, jj, la: (b, 0, jj[p], 0)),
            pl.BlockSpec((1, HEAD_DIM), small),
            pl.BlockSpec((1, HEAD_DIM), small),
            pl.BlockSpec((1, HEAD_DIM), small),
            pl.BlockSpec((1, HEAD_DIM), small),
            pl.BlockSpec((1, DIFF_V_DIM), small),
        ],
        out_specs=pl.BlockSpec((None, nh, tq, LANES), lambda b, p, ii, jj, la: (b, 0, ii[p], 0)),
        scratch_shapes=[pltpu.VMEM((2 * nh, tq, LANES), F32), pltpu.VMEM((2 * nh, tq, 2 * LANES), F32)],
    )
    return pl.pallas_call(
        functools.partial(_diff_attn_kernel, tq=tq, tk=tk, qoff=qoff),
        out_shape=jax.ShapeDtypeStruct((nbatch, nh, lq, LANES), BF16),
        grid_spec=grid_spec,
        compiler_params=_cparams(("arbitrary", "arbitrary")),
        name="diff_attn",
    )(ii, jj, last, q, k, v, lq1, lk1, lq2, lk2, subln_g)


def _post_kernel(oa_ref, ob_ref, sga_ref, sgb_ref, x_ref, gta_ref, shm_ref, scm_ref, g_ref,
                 wpa_ref, wpb_ref, wo_ref, wq_ref, x1_ref, h2_ref, q_ref):
    oa = jnp.concatenate([oa_ref[h] for h in range(N_HEADS_FOX)], axis=1)
    ob = jnp.concatenate([ob_ref[h] for h in range(N_HEADS_DIFF)], axis=1)
    ya = jnp.dot(oa, wpa_ref[...], preferred_element_type=F32)
    yb = jnp.dot(ob, wpb_ref[...], preferred_element_type=F32)
    mix_in = sga_ref[...].astype(F32) * ya + sgb_ref[...].astype(F32) * yb
    mix = jnp.dot(mix_in.astype(BF16), wo_ref[...], preferred_element_type=F32)
    x1 = x_ref[...] + gta_ref[...] * mix
    x1_ref[...] = x1
    h2 = _modnorm(x1, g_ref[...], scm_ref[...], shm_ref[...])
    h2_ref[...] = h2.astype(BF16)
    q_ref[...] = jnp.dot(h2, wq_ref[...], precision=HIGHEST, preferred_element_type=F32)


def _post_call(oa, ob, sga, sgb, x2d, gta, shm, scm, g, w_pa, w_pb, w_o, w_q, *, tb, per_block_mod):
    n = x2d.shape[0]
    nb = n // tb
    mod_map = (lambda i: (i, 0, 0)) if per_block_mod else (lambda i: (0, 0, 0))
    const2 = lambda i: (0, 0)
    row = lambda i: (i, 0)
    return pl.pallas_call(
        _post_kernel,
        out_shape=(
            jax.ShapeDtypeStruct((n, D_MODEL), F32),
            jax.ShapeDtypeStruct((n, D_MODEL), BF16),
            jax.ShapeDtypeStruct((n, D_MODEL), F32),
        ),
        grid=(nb,),
        in_specs=[
            pl.BlockSpec((N_HEADS_FOX, tb, LANES), lambda i: (0, i, 0)),
            pl.BlockSpec((N_HEADS_DIFF, tb, LANES), lambda i: (0, i, 0)),
            pl.BlockSpec((tb, D_MODEL), row),
            pl.BlockSpec((tb, D_MODEL), row),
            pl.BlockSpec((tb, D_MODEL), row),
            pl.BlockSpec((None, 1, D_MODEL), mod_map),
            pl.BlockSpec((None, 1, D_MODEL), mod_map),
            pl.BlockSpec((None, 1, D_MODEL), mod_map),
            pl.BlockSpec((1, D_MODEL), const2),
            pl.BlockSpec(w_pa.shape, const2),
            pl.BlockSpec(w_pb.shape, const2),
            pl.BlockSpec(w_o.shape, const2),
            pl.BlockSpec(w_q.shape, const2),
        ],
        out_specs=(
            pl.BlockSpec((tb, D_MODEL), row),
            pl.BlockSpec((tb, D_MODEL), row),
            pl.BlockSpec((tb, D_MODEL), row),
        ),
        compiler_params=_cparams(("arbitrary",)),
        name="post_attn",
    )(oa, ob, sga, sgb, x2d, gta, shm, scm, g, w_pa, w_pb, w_o, w_q)


def _extract_top(xs, n_keep, on_round=None, *, exact_ties):
    rids = [lax.broadcasted_iota(jnp.int32, x.shape, 0) for x in xs]

    def body(t, carry):
        out = []
        for k, xc in enumerate(carry):
            m = jnp.max(xc, axis=0, keepdims=True)
            hit = xc == m
            if exact_ties:
                first = jnp.min(jnp.where(hit, rids[k], xc.shape[0]), axis=0, keepdims=True)
                hit = rids[k] == first
            if on_round is not None:
                on_round(k, t, m, hit)
            out.append(jnp.where(hit, -jnp.inf, xc))
        return tuple(out)

    return lax.fori_loop(0, n_keep, body, tuple(xs))


def _peer_router(h2_ref, q_ref, sk_ref, h2t_ref, cnt_ref, e1_ref, rank2_ref, e2_ref,
                 qt_ref, sraw_ref, rank_ref, vbuf_ref):
    tb = h2_ref.shape[0]
    h2t_ref[...] = h2_ref[...].astype(F32).T.astype(BF16)
    qt_ref[...] = q_ref[...].T

    def half_body(hp, c):
        qs = qt_ref[pl.ds(pl.multiple_of(hp * PEER_KEY_HALF, PEER_KEY_HALF), PEER_KEY_HALF), :]
        sraw_ref[hp] = jnp.dot(sk_ref[hp], qs, precision=HIGHEST, preferred_element_type=F32)
        return c

    lax.fori_loop(0, 2 * PEER_HEADS, half_body, 0)
    rank_ref[...] = jnp.full(rank_ref.shape, NOT_RANKED, F32)

    rid8 = lax.broadcasted_iota(jnp.int32, (SUBLANES, LANES), 0)

    def candidates(h, lt):
        v1 = vbuf_ref[2 * h, lt]
        v2 = vbuf_ref[2 * h + 1, lt]
        pieces = [v1[0:1, :] + v2]
        for a in range(1, PEER_TOPK):
            nb = PEER_TOPK // (a + 1)
            piece = v1[a:a + 1, :] + v2[0:SUBLANES, :]
            pieces.append(piece if nb >= SUBLANES else jnp.where(rid8 < nb, piece, -jnp.inf))
        return jnp.concatenate(pieces, axis=0), v1[0:1, :], v2[0:1, :]

    def finish_head(h, cols, cand, cand_end, top1, top2):
        chosen = cand_end != cand
        z = jnp.sum(jnp.where(chosen, jnp.exp(cand - cand[0:1, :]), 0.0), axis=0, keepdims=True)
        chosen_f = chosen.astype(F32)
        rank1 = rank_ref[2 * h, :, cols]
        cnt1 = jnp.zeros(rank1.shape, F32)
        n_chosen = jnp.zeros((1, LANES), F32)
        for a in range(PEER_TOPK):
            lo = 0 if a == 0 else PEER_TOPK + (a - 1) * SUBLANES
            hi = PEER_TOPK if a == 0 else lo + SUBLANES
            cnt_a = jnp.sum(chosen_f[lo:hi, :], axis=0, keepdims=True)
            n_chosen = n_chosen + cnt_a
            cnt1 = jnp.where(rank1 == float(a), cnt_a, cnt1)
        cnt_ref[h, :, cols] = cnt1
        e1_ref[h, :, cols] = jnp.exp(sraw_ref[2 * h, :, cols] - top1) * (1.0 / z)
        rank2_ref[h, :, cols] = rank_ref[2 * h + 1, :, cols].astype(BF16)
        e2_ref[h, :, cols] = jnp.exp(sraw_ref[2 * h + 1, :, cols] - top2).astype(BF16)
        return n_chosen

    def route_tile(lt, exact_ties):
        cols = slice(lt * LANES, (lt + 1) * LANES)

        def rank_body(h, excess):
            def on_round(k, t, top_row, hit):
                hp = 2 * h + k
                vbuf_ref[hp, lt, pl.ds(t, 1), :] = top_row
                rank_ref[hp, :, cols] = jnp.where(hit, t.astype(F32), rank_ref[hp, :, cols])
            ends = _extract_top([sraw_ref[2 * h, :, cols], sraw_ref[2 * h + 1, :, cols]], PEER_TOPK, on_round,
                                exact_ties=exact_ties)
            for end in ends:
                removed = jnp.sum((end == -jnp.inf).astype(F32), axis=0, keepdims=True)
                excess = excess + jnp.abs(removed - PEER_TOPK)
            return excess

        excess = lax.fori_loop(0, PEER_HEADS, rank_body, jnp.zeros((1, LANES), F32))

        def pair_body(hh, excess):
            heads = (2 * hh, 2 * hh + 1)
            cands = [candidates(h, lt) for h in heads]
            ends = _extract_top([cd[0] for cd in cands], PEER_TOPK, exact_ties=exact_ties)
            for h, cd, end in zip(heads, cands, ends):
                n_chosen = finish_head(h, cols, cd[0], end, cd[1], cd[2])
                excess = excess + jnp.abs(n_chosen - PEER_TOPK)
            return excess

        return lax.fori_loop(0, PEER_HEADS // 2, pair_body, excess)

    for lt in range(tb // LANES):
        excess = route_tile(lt, exact_ties=False)

        @pl.when(jnp.max(excess) > 0.0)
        def _():
            rank_ref[:, :, lt * LANES:(lt + 1) * LANES] = jnp.full(
                (2 * PEER_HEADS, PEER_N_KEYS, LANES), NOT_RANKED, F32)
            route_tile(lt, exact_ties=True)


def _peer_gate_coeff(i1_base, st_ref, coeff_ref, cnt_ref, e1_ref, rank2_ref, e2_ref):
    tb = st_ref.shape[1]
    pack = 2 * SUBLANES
    n_tiles = PEER_N_KEYS // pack
    for kf in range(PEER_SUB_BLOCK // PEER_N_KEYS):
        i1 = i1_base + kf
        gate = [jnp.zeros((pack, tb), BF16) for _ in range(n_tiles)]
        for h in range(PEER_HEADS):
            cnt_b = jnp.broadcast_to(cnt_ref[h, pl.ds(i1, 1), :], (pack, tb)).astype(BF16)
            e1_b = jnp.broadcast_to(e1_ref[h, pl.ds(i1, 1), :], (pack, tb)).astype(BF16)
            for rt in range(n_tiles):
                rows = slice(rt * pack, (rt + 1) * pack)
                gate[rt] = gate[rt] + jnp.where(rank2_ref[h, rows, :] < cnt_b, e2_ref[h, rows, :] * e1_b,
                                                jnp.zeros((pack, tb), BF16))
        for rt in range(n_tiles):
            rows = slice(kf * PEER_N_KEYS + rt * pack, kf * PEER_N_KEYS + (rt + 1) * pack)
            s = st_ref[rows, :]
            act = 0.5 * s * (1.0 + lax.erf(s * (2.0 ** -0.5)))
            coeff_ref[rows, :] = act.astype(BF16) * gate[rt]


def _peer_kernel(h2_ref, q_ref, sk_ref, u_ref, vt_ref, o_ref,
                 h2t_ref, cnt_ref, e1_ref, rank2_ref, e2_ref, st_ref, coeff_ref, acc_ref, *, eb):
    e = pl.program_id(1)
    ne = pl.num_programs(1)
    tb = h2_ref.shape[0]

    @pl.when(e == 0)
    def _():
        pl.run_scoped(
            functools.partial(_peer_router, h2_ref, q_ref, sk_ref, h2t_ref, cnt_ref, e1_ref, rank2_ref, e2_ref),
            pltpu.VMEM((D_MODEL, tb), F32),
            pltpu.VMEM((2 * PEER_HEADS, PEER_N_KEYS, tb), F32),
            pltpu.VMEM((2 * PEER_HEADS, PEER_N_KEYS, tb), F32),
            pltpu.VMEM((2 * PEER_HEADS, tb // LANES, PEER_TOPK, LANES), F32))
        acc_ref[...] = jnp.zeros(acc_ref.shape, F32)

    h2t = h2t_ref[...]
    n_sub = eb // PEER_SUB_BLOCK

    def scores(sb):
        lo = sb * PEER_SUB_BLOCK
        st_ref[sb % 2] = jnp.dot(u_ref[lo:lo + PEER_SUB_BLOCK, :], h2t, preferred_element_type=F32)

    scores(0)
    for sb in range(n_sub):
        if sb + 1 < n_sub:
            scores(sb + 1)
        i1_base = e * (eb // PEER_N_KEYS) + sb * (PEER_SUB_BLOCK // PEER_N_KEYS)
        _peer_gate_coeff(i1_base, st_ref.at[sb % 2], coeff_ref.at[sb % 2], cnt_ref, e1_ref, rank2_ref, e2_ref)
        lo = sb * PEER_SUB_BLOCK
        acc_ref[...] += jnp.dot(vt_ref[:, lo:lo + PEER_SUB_BLOCK], coeff_ref[sb % 2],
                                preferred_element_type=F32)

    @pl.when(e == ne - 1)
    def _():
        o_ref[...] = acc_ref[...].T


def _peer_call(h2, q, subkeys, u_b, vt_b, *, tb, eb):
    n = h2.shape[0]
    n_exp = u_b.shape[0]
    nt = n // tb
    ne = n_exp // eb
    return pl.pallas_call(
        functools.partial(_peer_kernel, eb=eb),
        out_shape=jax.ShapeDtypeStruct((n, D_MODEL), F32),
        grid=(nt, ne),
        in_specs=[
            pl.BlockSpec((tb, D_MODEL), lambda t, e: (t, 0)),
            pl.BlockSpec((tb, D_MODEL), lambda t, e: (t, 0)),
            pl.BlockSpec(subkeys.shape, lambda t, e: (0, 0, 0)),
            pl.BlockSpec((eb, D_MODEL), lambda t, e: (e, 0)),
            pl.BlockSpec((D_MODEL, eb), lambda t, e: (0, e)),
        ],
        out_specs=pl.BlockSpec((tb, D_MODEL), lambda t, e: (t, 0)),
        scratch_shapes=[
            pltpu.VMEM((D_MODEL, tb), BF16),
            pltpu.VMEM((PEER_HEADS, PEER_N_KEYS, tb), F32),
            pltpu.VMEM((PEER_HEADS, PEER_N_KEYS, tb), F32),
            pltpu.VMEM((PEER_HEADS, PEER_N_KEYS, tb), BF16),
            pltpu.VMEM((PEER_HEADS, PEER_N_KEYS, tb), BF16),
            pltpu.VMEM((2, PEER_SUB_BLOCK, tb), F32),
            pltpu.VMEM((2, PEER_SUB_BLOCK, tb), BF16),
            pltpu.VMEM((D_MODEL, tb), F32),
        ],
        compiler_params=_cparams(("arbitrary", "arbitrary")),
        name="peer",
    )(h2, q, subkeys, u_b, vt_b)


def _final_kernel(x1_ref, peer_ref, gtm_ref, g_ref, y_ref):
    x2 = x1_ref[...] + gtm_ref[...] * peer_ref[...]
    ms = jnp.mean(x2 * x2, axis=-1, keepdims=True)
    y_ref[...] = x2 * lax.rsqrt(ms + NORM_EPS) * g_ref[...]


def _final_call(x1, peer_all, gtm, g_final, *, tb, block_off, per_block_mod):
    n = x1.shape[0]
    nb = n // tb
    mod_map = (lambda i: (i, 0, 0)) if per_block_mod else (lambda i: (0, 0, 0))
    return pl.pallas_call(
        _final_kernel,
        out_shape=jax.ShapeDtypeStruct((n, D_MODEL), F32),
        grid=(nb,),
        in_specs=[
            pl.BlockSpec((tb, D_MODEL), lambda i: (i, 0)),
            pl.BlockSpec((tb, D_MODEL), lambda i: (i + block_off, 0)),
            pl.BlockSpec((None, 1, D_MODEL), mod_map),
            pl.BlockSpec((1, D_MODEL), lambda i: (0, 0)),
        ],
        out_specs=pl.BlockSpec((tb, D_MODEL), lambda i: (i, 0)),
        compiler_params=_cparams(("arbitrary",)),
        name="final_norm",
    )(x1, peer_all, gtm, g_final)


def _rope_tables(pos):
    half = HEAD_DIM // 2
    inv = ROPE_THETA ** (-jnp.arange(half, dtype=F32) / half)
    ang = pos.astype(F32)[:, None] * inv[None, :]
    cos = jnp.cos(ang)
    sin = jnp.sin(ang)
    cos_t = jnp.concatenate([cos, cos, cos, cos], axis=1)
    sin_t = jnp.concatenate([-sin, sin, -sin, sin], axis=1)
    return cos_t, sin_t


def _pad_cols(w, width):
    return jnp.pad(w, ((0, 0), (0, width - w.shape[1])))


def kernel(x_prompt, x_sample, cache_fox_k, cache_fox_v, cache_fox_logf, cache_diff_k, cache_diff_v, c_prompt, c_sample, w_ada, b_ada, g_attn, w_in, b_f, diff_lq1, diff_lk1, diff_lq2, diff_lk2, diff_subln_g, w_pa, w_pb, w_o, g_ffn, peer_wq, peer_subkeys, peer_u, peer_v, g_final):
    nb_p, seq, d = x_prompt.shape
    nb_s, dec_seq, _ = x_sample.shape
    past = cache_fox_k.shape[2]
    assert nb_p == 1 and d == D_MODEL and w_ada.shape[0] == 1
    assert seq % ATTN_BLOCK == 0 and past % ATTN_BLOCK == 0 and dec_seq % CHUNK == 0 and dec_seq <= ATTN_BLOCK
    n_s = nb_s * dec_seq
    assert (seq + n_s) % PEER_TOKEN_BLOCK == 0

    w_in0 = w_in[0]
    o_fq, o_fk, o_fv = 0, FOX_WIDTH, 2 * FOX_WIDTH
    o_ff = 3 * FOX_WIDTH
    o_dq = o_ff + N_HEADS_FOX
    o_dk = o_dq + DIFF_QK_WIDTH
    o_dv = o_dk + DIFF_QK_WIDTH
    o_ga = o_dv + DIFF_V_WIDTH
    w_fox = jnp.concatenate([w_in0[:, o_fq:o_fk] * ATTN_SCALE, w_in0[:, o_fk:o_ff]], axis=1).astype(BF16)
    w_ff = _pad_cols(w_in0[:, o_ff:o_dq], LANES)
    b_f_pad = _pad_cols(b_f, LANES)
    w_dq = (w_in0[:, o_dq:o_dk] * ATTN_SCALE).reshape(D_MODEL, N_HEADS_DIFF, 2, HEAD_DIM)
    zeros_q = jnp.zeros((D_MODEL, N_HEADS_DIFF, HEAD_DIM), F32)
    w_dq_pad = jnp.stack([
        jnp.concatenate([w_dq[:, :, 0], zeros_q], axis=-1),
        jnp.concatenate([zeros_q, w_dq[:, :, 1]], axis=-1)], axis=2).reshape(D_MODEL, 2 * DIFF_QK_WIDTH)
    w_dg = jnp.concatenate([w_dq_pad, w_in0[:, o_dk:]], axis=1).astype(BF16)
    w_pa_pad = jnp.pad(w_pa[0].reshape(N_HEADS_FOX, HEAD_DIM, D_MODEL),
                       ((0, 0), (0, LANES - HEAD_DIM), (0, 0))).reshape(N_HEADS_FOX * LANES, D_MODEL).astype(BF16)
    w_pb_b = w_pb[0].astype(BF16)
    w_o_b = w_o[0].astype(BF16)
    w_q = peer_wq[0]
    subkeys = peer_subkeys[0].reshape(2 * PEER_HEADS, PEER_N_KEYS, PEER_KEY_HALF)
    u_b = peer_u[0].astype(BF16)
    vt_b = peer_v[0].T.astype(BF16)
    consts = _fox_aug_consts()
    g_attn2 = g_attn
    g_ffn2 = g_ffn
    g_final2 = g_final.reshape(1, D_MODEL)

    n_rows = 1 + nb_s
    n_rows_pad = -(-n_rows // SUBLANES) * SUBLANES
    c_rows = jnp.pad(jnp.concatenate([c_prompt, c_sample], axis=0), ((0, n_rows_pad - n_rows), (0, 0)))
    mod = _ada_call(c_rows, w_ada[0], b_ada).reshape(n_rows_pad, 6, 1, D_MODEL)
    mod_p = [mod[0:1, k] for k in range(6)]
    mod_s = [mod[1:n_rows, k] for k in range(6)]

    xp = x_prompt.reshape(seq, D_MODEL)
    xs = x_sample.reshape(n_s, D_MODEL)

    fk_p, fv_p, logf_p, qa_p, ka_p, va_p, base_p = _fox_pre_call(
        xp, mod_p[0], mod_p[1], g_attn2, w_fox, w_ff, b_f_pad, consts,
        tb=TOKEN_BLOCK, per_block_mod=False, carry=True)
    fk_s, fv_s, logf_s, qa_s, ka_s, va_s, _ = _fox_pre_call(
        xs, mod_s[0], mod_s[1], g_attn2, w_fox, w_ff, b_f_pad, consts,
        tb=dec_seq, per_block_mod=True, carry=False)
    cos_p, sin_p = _rope_tables(jnp.arange(seq, dtype=jnp.int32))
    cos_s1, sin_s1 = _rope_tables(jnp.arange(past, past + dec_seq, dtype=jnp.int32))
    cos_s = jnp.tile(cos_s1, (nb_s, 1))
    sin_s = jnp.tile(sin_s1, (nb_s, 1))
    dk_p, dv_p, dqp_p, dkb_p, dva_p, sga_p, sgb_p = _diff_pre_call(
        xp, mod_p[0], mod_p[1], g_attn2, w_dg, cos_p, sin_p, tb=TOKEN_BLOCK, per_block_mod=False)
    dk_s, dv_s, dqp_s, dkb_s, dva_s, sga_s, sgb_s = _diff_pre_call(
        xs, mod_s[0], mod_s[1], g_attn2, w_dg, cos_s, sin_s, tb=dec_seq, per_block_mod=True)

    nblk = seq // ATTN_BLOCK
    base_hp = base_p.reshape(nblk, LANES)[:, :N_HEADS_FOX].T.reshape(-1)
    oa_p = _fox_attn_call(qa_p[None], ka_p[None], va_p[None], base_hp, base_hp,
                          tq=ATTN_BLOCK, tk=ATTN_BLOCK, qoff=0)[0]

    ka_c, va_c, base_c, tot_c = _fox_cache_call(
        cache_fox_k[0].reshape(nb_s, past, FOX_WIDTH), cache_fox_v[0].reshape(nb_s, past, FOX_WIDTH),
        jnp.pad(cache_fox_logf[0], ((0, 0), (0, 0), (0, LANES - N_HEADS_FOX))), consts)
    kv_len = past + ATTN_BLOCK
    pad_rows = kv_len - past - dec_seq

    def per_stream(a):
        return a.reshape(a.shape[0], nb_s, dec_seq, a.shape[2]).transpose(1, 0, 2, 3)

    def with_new(cache_part, new_part):
        z = jnp.zeros(new_part.shape[:2] + (pad_rows, new_part.shape[3]), new_part.dtype)
        return jnp.concatenate([cache_part, new_part, z], axis=2)

    k_all = with_new(ka_c, per_stream(ka_s))
    v_all = with_new(va_c, per_stream(va_s))
    nr = past // ATTN_BLOCK
    base_c2 = base_c.reshape(nb_s, nr, LANES)[:, :, :N_HEADS_FOX]
    tot_c2 = tot_c.reshape(nb_s, nr, LANES)[:, :, :N_HEADS_FOX]
    base_new = base_c2[:, nr - 1:nr] + tot_c2[:, nr - 1:nr]
    kbase_s = jnp.concatenate([base_c2, base_new], axis=1).transpose(0, 2, 1)
    qbase_s = kbase_s[:, :, nr:]
    oa_s = _fox_attn_call(per_stream(qa_s), k_all, v_all, qbase_s.reshape(-1), kbase_s.reshape(-1),
                          tq=dec_seq, tk=ATTN_BLOCK, qoff=past)
    oa_s = oa_s.transpose(1, 0, 2, 3).reshape(N_HEADS_FOX, n_s, LANES)

    lqk = (diff_lq1, diff_lk1, diff_lq2, diff_lk2, diff_subln_g)
    ob_p = _diff_attn_call(dqp_p[None], dkb_p[None], dva_p[None], *lqk,
                           tq=ATTN_BLOCK, tk=ATTN_BLOCK, qoff=0)[0]
    cdk = cache_diff_k[0].reshape(nb_s, past, N_HEADS_DIFF, LANES).transpose(0, 2, 1, 3).astype(BF16)
    cdv = cache_diff_v[0].transpose(0, 2, 1, 3).astype(BF16)
    ones_col = jnp.zeros((nb_s, N_HEADS_DIFF, past, LANES), BF16).at[..., 0].set(1.0)
    cdv_aug = jnp.concatenate([cdv, ones_col], axis=-1)
    dk_all = with_new(cdk, per_stream(dkb_s))
    dv_all = with_new(cdv_aug, per_stream(dva_s))
    ob_s = _diff_attn_call(per_stream(dqp_s), dk_all, dv_all, *lqk,
                           tq=dec_seq, tk=ATTN_BLOCK, qoff=past)
    ob_s = ob_s.transpose(1, 0, 2, 3).reshape(N_HEADS_DIFF, n_s, LANES)

    x1_p, h2_p, q_p = _post_call(oa_p, ob_p, sga_p, sgb_p, xp, mod_p[2], mod_p[3], mod_p[4], g_ffn2,
                                 w_pa_pad, w_pb_b, w_o_b, w_q, tb=TOKEN_BLOCK, per_block_mod=False)
    x1_s, h2_s, q_s = _post_call(oa_s, ob_s, sga_s, sgb_s, xs, mod_s[2], mod_s[3], mod_s[4], g_ffn2,
                                 w_pa_pad, w_pb_b, w_o_b, w_q, tb=dec_seq, per_block_mod=True)

    h2_all = jnp.concatenate([h2_p, h2_s], axis=0)
    q_all = jnp.concatenate([q_p, q_s], axis=0)
    peer_all = _peer_call(h2_all, q_all, subkeys, u_b, vt_b, tb=PEER_TOKEN_BLOCK, eb=PEER_EXPERT_BLOCK)

    y_p = _final_call(x1_p, peer_all, mod_p[5], g_final2, tb=TOKEN_BLOCK, block_off=0, per_block_mod=False)
    y_s = _final_call(x1_s, peer_all, mod_s[5], g_final2, tb=dec_seq, block_off=seq // dec_seq, per_block_mod=True)

    dt = x_prompt.dtype
    return (
        y_p.reshape(1, seq, D_MODEL),
        y_s.reshape(nb_s, dec_seq, D_MODEL),
        fk_p.reshape(1, 1, seq, N_HEADS_FOX, HEAD_DIM),
        fv_p.reshape(1, 1, seq, N_HEADS_FOX, HEAD_DIM),
        logf_p.astype(dt).reshape(1, 1, seq, N_HEADS_FOX),
        dk_p.reshape(1, 1, seq, N_HEADS_DIFF, 2, HEAD_DIM),
        dv_p.reshape(1, 1, seq, N_HEADS_DIFF, DIFF_V_DIM),
        fk_s.reshape(1, nb_s, dec_seq, N_HEADS_FOX, HEAD_DIM),
        fv_s.reshape(1, nb_s, dec_seq, N_HEADS_FOX, HEAD_DIM),
        logf_s.astype(dt).reshape(1, nb_s, dec_seq, N_HEADS_FOX),
        dk_s.reshape(1, nb_s, dec_seq, N_HEADS_DIFF, 2, HEAD_DIM),
        dv_s.reshape(1, nb_s, dec_seq, N_HEADS_DIFF, DIFF_V_DIM),
    )
```
